```python
import jax, jax.numpy as jnp
from jax import lax
import numpy as np

D_MODEL = 1024
BATCH = 32
SEQ = 2048
DEPTH = 1
DEC_BATCH = 128
DEC_SEQ = 8
PAST_LEN = 8192
PAGE_SIZE = 128

D_POOL = D_MODEL // 2
POOL_WINDOWS = (2, 4, 8, 16)
N_POOL_GROUPS = len(POOL_WINDOWS)
POOL_GROUP = D_POOL // N_POOL_GROUPS
POOL_BUF = max(POOL_WINDOWS) - 1
N_HEADS = 8
HEAD_DIM = 64
D_ATTN = N_HEADS * HEAD_DIM
IDX_HEADS = 8
IDX_DIM = 64
INDEX_TOPK = 256
Q_BLOCK = 128
ROPE_THETA = 10000.0
D_FF = (8 * D_MODEL // 3 + 127) // 128 * 128
CONV_W = 3
N_BRANCH = 2
RMS_EPS = 1e-6

OFF_POOL = 0
OFF_Q = OFF_POOL + D_POOL
OFF_K = OFF_Q + D_ATTN
OFF_V = OFF_K + D_ATTN
OFF_QI = OFF_V + D_ATTN
OFF_KI = OFF_QI + IDX_HEADS * IDX_DIM
OFF_IW = OFF_KI + IDX_DIM
OFF_G = OFF_IW + IDX_HEADS
D_IN = OFF_G + N_BRANCH * D_MODEL

kernel_name = "hybrid_pool_dsa_convglu_step"

F32 = jnp.float32


def rms_norm(x, g):
    xf = x.astype(F32)
    y = xf * lax.rsqrt(jnp.mean(xf * xf, axis=-1, keepdims=True) + RMS_EPS)
    return (y * g.astype(F32)).astype(x.dtype)


def rope(x, pos):
    half = x.shape[-1] // 2
    freq = ROPE_THETA ** (-jnp.arange(half, dtype=F32) / half)
    ang = pos.astype(F32)[:, None] * freq[None, :]
    cos = jnp.cos(ang)[:, None, :]
    sin = jnp.sin(ang)[:, None, :]
    xf = x.astype(F32)
    x1, x2 = xf[..., :half], xf[..., half:]
    return jnp.concatenate([x1 * cos - x2 * sin, x2 * cos + x1 * sin], axis=-1).astype(x.dtype)


def mixer_inputs(h, w_in, b_gate, pos):
    z = h @ w_in
    B, L, _ = z.shape
    p = z[..., OFF_POOL:OFF_Q]
    q = rope(z[..., OFF_Q:OFF_K].reshape(B, L, N_HEADS, HEAD_DIM), pos)
    k = rope(z[..., OFF_K:OFF_V].reshape(B, L, N_HEADS, HEAD_DIM), pos)
    v = z[..., OFF_V:OFF_QI].reshape(B, L, N_HEADS, HEAD_DIM)
    qi = rope(z[..., OFF_QI:OFF_KI].reshape(B, L, IDX_HEADS, IDX_DIM), pos)
    ki = rope(z[..., OFF_KI:OFF_IW][:, :, None, :], pos)[:, :, 0]
    iw = z[..., OFF_IW:OFF_G] * (IDX_HEADS ** -0.5)
    gates = jax.nn.sigmoid((z[..., OFF_G:] + b_gate).astype(F32)).astype(h.dtype)
    return p, q, k, v, qi, ki, iw, gates


def pool_mix(p, prefix, pos, mix_w, scale):
    B, L, _ = p.shape
    ext = jnp.concatenate([prefix.astype(p.dtype), p], axis=1)
    c = jnp.cumsum(ext.astype(F32), axis=1)
    c = jnp.concatenate([jnp.zeros_like(c[:, :1]), c], axis=1)
    end = c[:, POOL_BUF + 1:]
    means = []
    for g, w in enumerate(POOL_WINDOWS):
        sl = slice(g * POOL_GROUP, (g + 1) * POOL_GROUP)
        start = POOL_BUF + 1 - w
        wsum = end[..., sl] - c[:, start:start + L, sl]
        cnt = jnp.minimum(w, pos + 1).astype(F32)[:, None]
        means.append(wsum / cnt)
    d = (jnp.concatenate(means, axis=-1) - p.astype(F32)).reshape(B, L, N_POOL_GROUPS, POOL_GROUP)
    out = jnp.einsum('blgc,gcd->blgd', d, mix_w.astype(F32)).reshape(B, L, D_POOL) * scale.astype(F32)
    return out.astype(p.dtype), ext[:, -POOL_BUF:]


def index_select(qi, ki, iw, q_pos, topk):
    dots = jnp.einsum('qhd,sd->qhs', qi.astype(F32), ki.astype(F32)) * (IDX_DIM ** -0.5)
    score = jnp.einsum('qh,qhs->qs', iw.astype(F32), jax.nn.relu(dots))
    k_pos = jnp.arange(ki.shape[0])
    score = jnp.where(k_pos[None, :] <= q_pos[:, None], score, -jnp.inf)
    _, sel = lax.top_k(score, topk)
    return sel, sel <= q_pos[:, None]


def sparse_attend(q, k_sel, v_sel, valid):
    s = jnp.einsum('qhd,qkhd->qhk', q.astype(F32), k_sel.astype(F32)) * (HEAD_DIM ** -0.5)
    s = jnp.where(valid[:, None, :], s, -jnp.inf)
    pr = jax.nn.softmax(s, axis=-1)
    return jnp.einsum('qhk,qkhd->qhd', pr, v_sel.astype(F32)).astype(q.dtype)


def prompt_attention(q, k, v, qi, ki, iw):
    B, L = q.shape[:2]
    topk = min(INDEX_TOPK, L // 4)
    nb = L // Q_BLOCK

    def per_seq(args):
        qb, kb, vb, qib, kib, iwb = args

        def per_block(bargs):
            qq, qqi, qiw, qpos = bargs
            sel, valid = index_select(qqi, kib, qiw, qpos, topk)
            return sparse_attend(qq, kb[sel], vb[sel], valid)

        blocks = (qb.reshape(nb, Q_BLOCK, N_HEADS, HEAD_DIM),
                  qib.reshape(nb, Q_BLOCK, IDX_HEADS, IDX_DIM),
                  iwb.reshape(nb, Q_BLOCK, IDX_HEADS),
                  jnp.arange(L).reshape(nb, Q_BLOCK))
        return lax.map(per_block, blocks).reshape(L, N_HEADS, HEAD_DIM)

    return lax.map(per_seq, (q, k, v, qi, ki, iw))


def sample_attention(q, k, v, qi, ki, iw, cache_k, cache_v, cache_kidx, page_table):
    Ln = q.shape[1]
    topk = min(INDEX_TOPK, (PAST_LEN + Ln) // 4)
    q_pos = PAST_LEN + jnp.arange(Ln)

    def per_seq(args):
        qb, kb, vb, qib, kib, iwb, pages = args
        ki_past = cache_kidx[pages].reshape(PAST_LEN, IDX_DIM)
        ki_all = jnp.concatenate([ki_past.astype(kib.dtype), kib], axis=0)
        sel, valid = index_select(qib, ki_all, iwb, q_pos, topk)
        from_past = (sel < PAST_LEN)[..., None, None]
        ps = jnp.minimum(sel, PAST_LEN - 1)
        phys = pages[ps // PAGE_SIZE]
        off = ps % PAGE_SIZE
        ns = jnp.clip(sel - PAST_LEN, 0, Ln - 1)
        k_sel = jnp.where(from_past, cache_k[phys, off].astype(kb.dtype), kb[ns])
        v_sel = jnp.where(from_past, cache_v[phys, off].astype(vb.dtype), vb[ns])
        return sparse_attend(qb, k_sel, v_sel, valid)

    return lax.map(per_seq, (q, k, v, qi, ki, iw, page_table))


def merge_branches(x, pool_out, attn_out, gates, w_pool_o, w_attn_o, w_out):
    B, L, _ = x.shape
    m = (gates[..., :D_MODEL] * (pool_out @ w_pool_o)
         + gates[..., D_MODEL:] * (attn_out.reshape(B, L, D_ATTN) @ w_attn_o))
    return x + m @ w_out


def conv_ffn(h, prefix, w_up, conv_w, conv_b, w_down):
    u = h @ w_up
    L = u.shape[1]
    ext = jnp.concatenate([prefix.astype(u.dtype), u], axis=1)
    c = conv_b + sum(conv_w[j] * ext[:, j:j + L] for j in range(CONV_W))
    gate, val = jnp.split(c, 2, axis=-1)
    return (jax.nn.gelu(gate) * val) @ w_down, ext[:, -(CONV_W - 1):]


def setup_inputs(seed: int = 0) -> dict:
    key = jax.random.key(seed)
    ks = jax.random.split(key, 24)
    n_pages = PAST_LEN // PAGE_SIZE
    n_used = DEC_BATCH * n_pages
    n_pool_pages = n_used + n_used // 4
    page_table = jax.random.permutation(ks[0], n_pool_pages)[:n_used].reshape(DEC_BATCH, n_pages).astype(jnp.int32)
    nrm = lambda k, s, sc: jax.random.normal(k, s, F32) * sc
    return {
        "x_prompt": nrm(ks[1], (BATCH, SEQ, D_MODEL), 1.0),
        "x_sample": nrm(ks[2], (DEC_BATCH, DEC_SEQ, D_MODEL), 1.0),
        "cache_k": nrm(ks[3], (DEPTH, n_pool_pages, PAGE_SIZE, N_HEADS, HEAD_DIM), 1.0),
        "cache_v": nrm(ks[4], (DEPTH, n_pool_pages, PAGE_SIZE, N_HEADS, HEAD_DIM), 1.0),
        "cache_kidx": nrm(ks[5], (DEPTH, n_pool_pages, PAGE_SIZE, IDX_DIM), 1.0),
        "state_pool": nrm(ks[6], (DEPTH, DEC_BATCH, POOL_BUF, D_POOL), 1.0),
        "state_conv": nrm(ks[7], (DEPTH, DEC_BATCH, CONV_W - 1, 2 * D_FF), 1.0),
        "page_table": page_table,
        "norm1_g": 1.0 + nrm(ks[8], (DEPTH, D_MODEL), 0.05),
        "w_in": nrm(ks[9], (DEPTH, D_MODEL, D_IN), D_MODEL ** -0.5),
        "b_gate": nrm(ks[10], (DEPTH, N_BRANCH * D_MODEL), 0.02),
        "pool_mix_w": nrm(ks[11], (DEPTH, N_POOL_GROUPS, POOL_GROUP, POOL_GROUP), POOL_GROUP ** -0.5),
        "pool_scale": 1.0 + nrm(ks[12], (DEPTH, D_POOL), 0.1),
        "w_pool_o": nrm(ks[13], (DEPTH, D_POOL, D_MODEL), D_POOL ** -0.5),
        "w_attn_o": nrm(ks[14], (DEPTH, D_ATTN, D_MODEL), D_ATTN ** -0.5),
        "w_out": nrm(ks[15], (DEPTH, D_MODEL, D_MODEL), D_MODEL ** -0.5),
        "norm2_g": 1.0 + nrm(ks[16], (DEPTH, D_MODEL), 0.05),
        "w_up": nrm(ks[17], (DEPTH, D_MODEL, 2 * D_FF), D_MODEL ** -0.5),
        "conv_w": nrm(ks[18], (DEPTH, CONV_W, 2 * D_FF), CONV_W ** -0.5),
        "conv_b": nrm(ks[19], (DEPTH, 2 * D_FF), 0.02),
        "w_down": nrm(ks[20], (DEPTH, D_FF, D_MODEL), D_FF ** -0.5),
        "normf_g": 1.0 + nrm(ks[21], (D_MODEL,), 0.05),
    }


def reference(x_prompt, x_sample, cache_k, cache_v, cache_kidx, state_pool, state_conv, page_table,
              norm1_g, w_in, b_gate, pool_mix_w, pool_scale, w_pool_o, w_attn_o, w_out,
              norm2_g, w_up, conv_w, conv_b, w_down, normf_g):
    pos_p = jnp.arange(SEQ)
    pos_s = PAST_LEN + jnp.arange(DEC_SEQ)
    xp, xs = x_prompt, x_sample
    kp_l, vp_l, kip_l, poolp_l, convp_l = [], [], [], [], []
    ks_l, vs_l, kis_l, pools_l, convs_l = [], [], [], [], []
    for l in range(DEPTH):
        pp, qp, kp, vp, qip, kip, iwp, gp = mixer_inputs(rms_norm(xp, norm1_g[l]), w_in[l], b_gate[l], pos_p)
        ps, qs, ks, vs, qis, kis, iws, gs = mixer_inputs(rms_norm(xs, norm1_g[l]), w_in[l], b_gate[l], pos_s)
        pool_p, new_pool_p = pool_mix(pp, jnp.zeros((pp.shape[0], POOL_BUF, D_POOL), pp.dtype), pos_p,
                                      pool_mix_w[l], pool_scale[l])
        pool_s, new_pool_s = pool_mix(ps, state_pool[l], pos_s, pool_mix_w[l], pool_scale[l])
        attn_p = prompt_attention(qp, kp, vp, qip, kip, iwp)
        attn_s = sample_attention(qs, ks, vs, qis, kis, iws, cache_k[l], cache_v[l], cache_kidx[l], page_table)
        xp = merge_branches(xp, pool_p, attn_p, gp, w_pool_o[l], w_attn_o[l], w_out[l])
        xs = merge_branches(xs, pool_s, attn_s, gs, w_pool_o[l], w_attn_o[l], w_out[l])
        fp, new_conv_p = conv_ffn(rms_norm(xp, norm2_g[l]),
                                  jnp.zeros((xp.shape[0], CONV_W - 1, 2 * D_FF), xp.dtype),
                                  w_up[l], conv_w[l], conv_b[l], w_down[l])
        fs, new_conv_s = conv_ffn(rms_norm(xs, norm2_g[l]), state_conv[l],
                                  w_up[l], conv_w[l], conv_b[l], w_down[l])
        xp = xp + fp
        xs = xs + fs
        kp_l.append(kp); vp_l.append(vp); kip_l.append(kip); poolp_l.append(new_pool_p); convp_l.append(new_conv_p)
        ks_l.append(ks); vs_l.append(vs); kis_l.append(kis); pools_l.append(new_pool_s); convs_l.append(new_conv_s)
    y_prompt = rms_norm(xp, normf_g)
    y_sample = rms_norm(xs, normf_g)
    return (y_prompt, y_sample,
            jnp.stack(kp_l), jnp.stack(vp_l), jnp.stack(kip_l), jnp.stack(poolp_l), jnp.stack(convp_l),
            jnp.stack(ks_l), jnp.stack(vs_l), jnp.stack(kis_l), jnp.stack(pools_l), jnp.stack(convs_l))
```

```python
import functools
import math

import jax
import jax.numpy as jnp
from jax import lax
from jax.experimental import pallas as pl
from jax.experimental.pallas import tpu as pltpu

F32 = jnp.float32
BF16 = jnp.bfloat16
I32 = jnp.int32

LANES = 128
SUBLANES = 8
VMEM_LIMIT_BYTES = 56 * 1024 * 1024

N_HEADS = 8
HEAD_DIM = 64
D_ATTN = N_HEADS * HEAD_DIM
IDX_HEADS = 8
IDX_DIM = 64
D_POOL = 512
POOL_WINDOWS = (2, 4, 8, 16)
POOL_PAD = 16
CONV_PAD = 8
INDEX_TOPK = 256
Q_BLOCK = 128
KEY_CHUNK = 256
ROPE_THETA = 10000.0
RMS_EPS = 1e-6
MASK_BIAS = -1e30
INT_MIN = -(2 ** 31)

C_P, C_Q, C_K, C_V, C_QI = 0, 512, 1024, 1536, 2048
C_KI4 = 2560
C_IW = 2816
C_GA = 2944
C_GB = 3968
C_END = 4992


def _params():
    return pltpu.CompilerParams(dimension_semantics=("arbitrary", "arbitrary"),
                                vmem_limit_bytes=VMEM_LIMIT_BYTES)


def _rms(x, g):
    return x * lax.rsqrt(jnp.mean(x * x, axis=-1, keepdims=True) + RMS_EPS) * g


def _rope_blocks(z, cos, sin, first_half):
    outs = []
    for c in range(z.shape[1] // LANES):
        zb = z[:, c * LANES:(c + 1) * LANES]
        sw = jnp.where(first_half, pltpu.roll(zb, LANES - 32, 1), pltpu.roll(zb, 32, 1))
        outs.append(zb * cos + sw * sin)
    return outs[0] if len(outs) == 1 else jnp.concatenate(outs, axis=1)


def _bdot(a, b):
    return jnp.dot(a, b, preferred_element_type=F32)


def _dot_nt(a, b):
    return lax.dot_general(a, b, (((1,), (1,)), ((), ())), preferred_element_type=F32)


def _split(x):
    hi = x.astype(BF16)
    return hi, (x - hi.astype(F32)).astype(BF16)


def _mixer_in_kernel(x_ref, pre_ref, cos_ref, sin_ref, pos1_ref, g1_ref, w_ref, bg_ref, mixw_ref,
                     pscale_ref, wpo_ref,
                     pa_ref, gb_ref, q_ref, k_ref, kb_ref, v_ref, vx_ref, qi_ref, misc_ref, ki_ref,
                     ki4_ref, tail_ref, carry_ref, *, emit_vt):
    j = pl.program_id(1)
    bt, lt, d = x_ref.shape
    m = bt * lt
    narrow = pa_ref.dtype

    x = x_ref[...].reshape(m, d)
    h = _rms(x, g1_ref[...]).astype(BF16)

    def rows(ref):
        t = ref[...]
        if bt == 1:
            return t
        return jnp.broadcast_to(t[None], (bt, lt, LANES)).reshape(m, LANES)

    cos, sin, pos1 = rows(cos_ref), rows(sin_ref), rows(pos1_ref)
    lane = lax.broadcasted_iota(I32, (m, LANES), 1)
    first_half = (lane % 64) < 32

    def proj(c0, c1):
        return _bdot(h, w_ref[:, c0:c1])

    p = proj(C_P, C_Q)

    @pl.when(j == 0)
    def _():
        carry_ref[...] = pre_ref[...]

    ext = jnp.concatenate([carry_ref[...], p.reshape(bt, lt, D_POOL)], axis=1)
    tail = ext[:, lt:, :]
    carry_ref[...] = tail
    tail_ref[...] = tail
    e2 = ext.reshape(bt * (POOL_PAD + lt), D_POOL)
    pools = []
    for g, w in enumerate(POOL_WINDOWS):
        s = e2[:, g * LANES:(g + 1) * LANES]
        sh = 1
        while sh < w:
            s = s + pltpu.roll(s, sh, 0)
            sh *= 2
        wsum = s.reshape(bt, POOL_PAD + lt, LANES)[:, POOL_PAD:, :].reshape(m, LANES)
        dlt = wsum / jnp.minimum(pos1, float(w)) - p[:, g * LANES:(g + 1) * LANES]
        pools.append(_bdot(dlt.astype(BF16), mixw_ref[g]))
    pool = jnp.concatenate(pools, axis=1) * pscale_ref[...]
    a = _bdot(pool.astype(BF16), wpo_ref[...])

    bg = bg_ref[...]
    ga = jax.nn.sigmoid(proj(C_GA, C_GB) + bg[:, :d])
    pa_ref[...] = (ga * a).astype(narrow).reshape(bt, lt, d)
    gb = jax.nn.sigmoid(proj(C_GB, C_END) + bg[:, d:])
    gb_ref[...] = gb.astype(narrow).reshape(bt, lt, d)

    q = _rope_blocks(proj(C_Q, C_K), cos, sin, first_half)
    q_ref[...] = q.astype(narrow).reshape(bt, lt, D_ATTN)
    k = _rope_blocks(proj(C_K, C_V), cos, sin, first_half)
    k_ref[...] = k.reshape(bt, lt, D_ATTN)
    kb_ref[...] = k.astype(narrow).reshape(bt, lt, D_ATTN)
    v = proj(C_V, C_QI)
    v_ref[...] = v.reshape(bt, lt, D_ATTN)
    if emit_vt:
        vx_ref[...] = v.T.astype(narrow).reshape(1, 1, D_ATTN, lt)
    else:
        vx_ref[...] = v.astype(narrow).reshape(bt, lt, D_ATTN)
    qi = _rope_blocks(proj(C_QI, C_KI4), cos, sin, first_half)
    qi_ref[...] = qi.reshape(bt, lt, 512)
    ki4 = _rope_blocks(proj(C_KI4, C_IW), cos, sin, first_half)
    ki_ref[...] = ki4[:, :IDX_DIM].reshape(bt, lt, IDX_DIM)
    hi, lo = _split(ki4)
    ki4_ref[...] = jnp.concatenate([hi[:, :LANES], lo[:, LANES:]], axis=1).reshape(bt, lt, 256)
    misc_ref[...] = (proj(C_IW, C_GA) * (IDX_HEADS ** -0.5 * IDX_DIM ** -0.5)).reshape(bt, lt, LANES)


def _mixer_in(x, prefix16, cos, sin, pos1, g1, w_all, bg, mixw, pscale, wpo, *, bt, lt, emit_vt, narrow):
    b, l, d = x.shape
    nb, nj = b // bt, l // lt
    tok = lambda c: pl.BlockSpec((bt, lt, c), lambda i, j: (i, j, 0))
    full = lambda a: pl.BlockSpec(a.shape, lambda i, j: (0,) * a.ndim)
    tab = pl.BlockSpec((lt, LANES), lambda i, j: (j, 0))
    pre = pl.BlockSpec((bt, POOL_PAD, D_POOL), lambda i, j: (i, 0, 0))
    if emit_vt:
        vx_spec = pl.BlockSpec((1, 1, D_ATTN, lt), lambda i, j: (i, j, 0, 0))
        vx_shape = jax.ShapeDtypeStruct((b, nj, D_ATTN, lt), narrow)
    else:
        vx_spec = tok(D_ATTN)
        vx_shape = jax.ShapeDtypeStruct((b, l, D_ATTN), narrow)
    sds = lambda c, dt: jax.ShapeDtypeStruct((b, l, c), dt)
    out_shape = (sds(d, narrow), sds(d, narrow), sds(D_ATTN, narrow), sds(D_ATTN, F32), sds(D_ATTN, narrow),
                 sds(D_ATTN, F32), vx_shape, sds(512, F32), sds(LANES, F32), sds(IDX_DIM, F32),
                 sds(256, BF16), jax.ShapeDtypeStruct((b, POOL_PAD, D_POOL), F32))
    out_specs = (tok(d), tok(d), tok(D_ATTN), tok(D_ATTN), tok(D_ATTN), tok(D_ATTN), vx_spec, tok(512),
                 tok(LANES), tok(IDX_DIM), tok(256), pre)
    return pl.pallas_call(
        functools.partial(_mixer_in_kernel, emit_vt=emit_vt),
        grid=(nb, nj),
        in_specs=[tok(d), pre, tab, tab, tab,
                  full(g1), full(w_all), full(bg), full(mixw), full(pscale), full(wpo)],
        out_specs=out_specs,
        out_shape=out_shape,
        scratch_shapes=[pltpu.VMEM((bt, POOL_PAD, D_POOL), F32)],
        compiler_params=_params(),
        name="mixer_in",
    )(x, prefix16, cos, sin, pos1, g1, w_all, bg, mixw, pscale, wpo)


def _float_key(s):
    b = lax.bitcast_convert_type(s, I32)
    k = b ^ ((b >> 31) & 0x7FFFFFFF)
    return jnp.where(b == INT_MIN, 0, k)


def _count(key_ref, n_chunks, chunk, pred):
    width = key_ref.shape[1]

    def body(c, acc):
        r0 = pl.multiple_of(c * chunk, chunk)
        kc = key_ref[pl.ds(r0, chunk), :]
        hit = jnp.where(pred(kc, r0), 1, 0).astype(I32)
        return acc + jnp.sum(hit.reshape(chunk // SUBLANES, SUBLANES, width), axis=0)

    acc = lax.fori_loop(0, n_chunks, body, jnp.zeros((SUBLANES, width), I32))
    return jnp.sum(acc, axis=0, keepdims=True)


def _topk_select(key_ref, n_chunks, chunk, topk, idx_bits):
    width = key_ref.shape[1]

    def bit_body(t, ans):
        bit = jnp.left_shift(jnp.int32(1), 31 - t)
        cand = ans | bit
        ck = cand ^ INT_MIN
        cnt = _count(key_ref, n_chunks, chunk, lambda kc, r0: kc >= ck)
        return jnp.where(cnt >= topk, cand, ans)

    ans = lax.fori_loop(0, 32, bit_body, jnp.zeros((1, width), I32))
    thr = ans ^ INT_MIN
    n_gt = _count(key_ref, n_chunks, chunk, lambda kc, r0: kc > thr)
    n_eq = _count(key_ref, n_chunks, chunk, lambda kc, r0: kc == thr)
    need = topk - n_gt
    big = jnp.full((1, width), 2 ** idx_bits, I32)

    def tie_search():
        def idx_body(t, xs):
            cand = xs | jnp.left_shift(jnp.int32(1), idx_bits - 1 - t)

            def pred(kc, r0):
                row = r0 + lax.broadcasted_iota(I32, kc.shape, 0)
                return (kc == thr) & (row < cand)

            cnt = _count(key_ref, n_chunks, chunk, pred)
            return jnp.where(cnt < need, cand, xs)

        return lax.fori_loop(0, idx_bits, idx_body, jnp.zeros((1, width), I32))

    has_tie = jnp.max(n_eq - need) > 0
    cut = lax.cond(has_tie, tie_search, lambda: big)
    return thr, cut


def _prompt_attn_kernel(qi_ref, misc_ref, q_ref, ki4_ref, kb_ref, vt_ref, o_ref, key_ref, bias_ref,
                        *, topk):
    i = pl.program_id(1)
    qb = q_ref.shape[1]
    lk = ki4_ref.shape[1]
    chunk = vt_ref.shape[3]
    n_chunks = (i * qb + qb + chunk - 1) // chunk
    q_pos = i * qb + lax.broadcasted_iota(I32, (1, qb), 1)

    qhi, qlo = _split(qi_ref[0].T)
    iwt = misc_ref[0].T[:IDX_HEADS, :]

    def head_rhs(hh):
        a, b = qhi[hh * 64:(hh + 1) * 64], qlo[hh * 64:(hh + 1) * 64]
        return jnp.concatenate([a, b, a, b], axis=0)

    rhs = [jnp.concatenate([head_rhs(2 * jp), head_rhs(2 * jp + 1)], axis=1) for jp in range(4)]

    def score_body(c, carry):
        r0 = pl.multiple_of(c * chunk, chunk)
        kc = ki4_ref[0, pl.ds(r0, chunk), :]
        sc = jnp.zeros((chunk, qb), F32)
        for jp in range(4):
            dots = jnp.maximum(_bdot(kc, rhs[jp]), 0.0)
            sc = sc + dots[:, :qb] * iwt[2 * jp:2 * jp + 1, :] + dots[:, qb:] * iwt[2 * jp + 1:2 * jp + 2, :]
        row = r0 + lax.broadcasted_iota(I32, (chunk, qb), 0)
        key_ref[pl.ds(r0, chunk), :] = jnp.where(row <= q_pos, _float_key(sc), INT_MIN)
        return carry

    lax.fori_loop(0, n_chunks, score_body, 0)

    def searched():
        return _topk_select(key_ref, n_chunks, chunk, topk, int(math.log2(lk)))

    def everything():
        return (jnp.full((1, qb), INT_MIN, I32), jnp.full((1, qb), lk, I32))

    thr, cut = lax.cond((i * qb + qb) > topk, searched, everything)

    def bias_body(c, carry):
        r0 = pl.multiple_of(c * chunk, chunk)
        kc = key_ref[pl.ds(r0, chunk), :]
        row = r0 + lax.broadcasted_iota(I32, (chunk, qb), 0)
        keep = ((kc > thr) | ((kc == thr) & (row <= cut))) & (row <= q_pos)
        bias_ref[pl.ds(r0, chunk), :] = jnp.where(keep, 0.0, MASK_BIAS)
        return carry

    lax.fori_loop(0, n_chunks, bias_body, 0)

    qt = q_ref[0].astype(F32).T.astype(BF16)
    zero = jnp.zeros((HEAD_DIM, qb), BF16)
    outs = []
    for jp in range(4):
        h0, h1 = 2 * jp, 2 * jp + 1
        qbd = jnp.concatenate(
            [jnp.concatenate([qt[h0 * 64:(h0 + 1) * 64], zero], axis=1),
             jnp.concatenate([zero, qt[h1 * 64:(h1 + 1) * 64]], axis=1)], axis=0)

        def attn_body(c, carry, qbd=qbd, jp=jp):
            m_run, l_run, acc = carry
            r0 = pl.multiple_of(c * chunk, chunk)
            kc = kb_ref[0, pl.ds(r0, chunk), jp * LANES:(jp + 1) * LANES]
            bias = bias_ref[pl.ds(r0, chunk), :]
            s = _bdot(kc, qbd) + jnp.concatenate([bias, bias], axis=1)
            m_new = jnp.maximum(m_run, jnp.max(s, axis=0, keepdims=True))
            p = jnp.exp(s - m_new)
            corr = jnp.exp(m_run - m_new)
            l_new = l_run * corr + jnp.sum(p, axis=0, keepdims=True)
            vt = vt_ref[0, c, jp * LANES:(jp + 1) * LANES, :]
            acc = acc * corr + _bdot(vt, p.astype(BF16))
            return m_new, l_new, acc

        init = (jnp.full((1, 2 * qb), MASK_BIAS, F32), jnp.zeros((1, 2 * qb), F32),
                jnp.zeros((LANES, 2 * qb), F32))
        _, l_fin, acc = lax.fori_loop(0, n_chunks, attn_body, init)
        o = acc / l_fin
        outs.append(o[:HEAD_DIM, :qb])
        outs.append(o[HEAD_DIM:, qb:])
    o_ref[0] = jnp.concatenate(outs, axis=0).T.astype(o_ref.dtype)


def _prompt_attention(qi, misc, q, ki4, kb, vt, *, topk):
    b, l, _ = q.shape
    nq = l // Q_BLOCK
    blk = lambda c: pl.BlockSpec((1, Q_BLOCK, c), lambda i, j: (i, j, 0))
    seq = lambda c: pl.BlockSpec((1, l, c), lambda i, j: (i, 0, 0))
    return pl.pallas_call(
        functools.partial(_prompt_attn_kernel, topk=topk),
        grid=(b, nq),
        in_specs=[blk(512), blk(LANES), blk(D_ATTN), seq(256), seq(D_ATTN),
                  pl.BlockSpec((1,) + vt.shape[1:], lambda i, j: (i, 0, 0, 0))],
        out_specs=blk(D_ATTN),
        out_shape=jax.ShapeDtypeStruct((b, l, D_ATTN), BF16),
        scratch_shapes=[pltpu.VMEM((l, Q_BLOCK), I32), pltpu.VMEM((l, Q_BLOCK), F32)],
        compiler_params=_params(),
        name="prompt_attention",
    )(qi, misc, q, ki4, kb, vt)


def _gelu_tanh(x):
    return 0.5 * x * (1.0 + jnp.tanh(math.sqrt(2.0 / math.pi) * (x + 0.044715 * (x * x * x))))


def _output_kernel(x_ref, pa_ref, gb_ref, at_ref, pre_ref, wao_ref, wout_ref, g2_ref, wup_ref, cw_ref,
                   cb_ref, wdn_ref, gf_ref, y_ref, tail_ref, carry_ref, *, ff_chunk):
    j = pl.program_id(1)
    bt, lt, d = x_ref.shape
    m = bt * lt
    dff = wdn_ref.shape[0]

    x = x_ref[...].reshape(m, d)
    at = at_ref[...].reshape(m, D_ATTN).astype(BF16)
    mrg = (pa_ref[...].reshape(m, d).astype(F32)
           + gb_ref[...].reshape(m, d).astype(F32) * _bdot(at, wao_ref[...]))
    x1 = x + _bdot(mrg.astype(BF16), wout_ref[...])
    h2 = _rms(x1, g2_ref[...]).astype(BF16)

    @pl.when(j == 0)
    def _():
        carry_ref[...] = pre_ref[...]

    def conv(c0):
        u = _bdot(h2, wup_ref[:, c0:c0 + ff_chunk])
        ext = jnp.concatenate([carry_ref[:, :, c0:c0 + ff_chunk], u.reshape(bt, lt, ff_chunk)], axis=1)
        carry_ref[:, :, c0:c0 + ff_chunk] = ext[:, lt:, :]
        e2 = ext.reshape(bt * (CONV_PAD + lt), ff_chunk)

        def shifted(e):
            return e.reshape(bt, CONV_PAD + lt, ff_chunk)[:, CONV_PAD:, :].reshape(m, ff_chunk)

        cw = cw_ref[:, c0:c0 + ff_chunk]
        return (cb_ref[:, c0:c0 + ff_chunk] + cw[0:1] * shifted(pltpu.roll(e2, 2, 0))
                + cw[1:2] * shifted(pltpu.roll(e2, 1, 0)) + cw[2:3] * u)

    f = jnp.zeros((m, d), F32)
    for cc in range(dff // ff_chunk):
        gate = conv(cc * ff_chunk)
        val = conv(dff + cc * ff_chunk)
        act = (_gelu_tanh(gate) * val).astype(BF16)
        f = f + _bdot(act, wdn_ref[cc * ff_chunk:(cc + 1) * ff_chunk, :])
    tail_ref[...] = carry_ref[...]
    y_ref[...] = _rms(x1 + f, gf_ref[...]).reshape(bt, lt, d)


def _output_stage(x, pa, gb, attn, prefix8, wao, wout, g2, wup, cw, cb, wdn, gf, *, bt, lt, ff_chunk):
    b, l, d = x.shape
    nb, nj = b // bt, l // lt
    c2 = wup.shape[1]
    tok = lambda c: pl.BlockSpec((bt, lt, c), lambda i, j: (i, j, 0))
    full = lambda a: pl.BlockSpec(a.shape, lambda i, j: (0,) * a.ndim)
    pre = pl.BlockSpec((bt, CONV_PAD, c2), lambda i, j: (i, 0, 0))
    return pl.pallas_call(
        functools.partial(_output_kernel, ff_chunk=ff_chunk),
        grid=(nb, nj),
        in_specs=[tok(d), tok(d), tok(d), tok(D_ATTN), pre, full(wao), full(wout), full(g2), full(wup),
                  full(cw), full(cb), full(wdn), full(gf)],
        out_specs=(tok(d), pre),
        out_shape=(jax.ShapeDtypeStruct((b, l, d), F32), jax.ShapeDtypeStruct((b, CONV_PAD, c2), F32)),
        scratch_shapes=[pltpu.VMEM((bt, CONV_PAD, c2), F32)],
        compiler_params=_params(),
        name="output_stage",
    )(x, pa, gb, attn, prefix8, wao, wout, g2, wup, cw, cb, wdn, gf)


def _sample_index_kernel(pt_ref, qi_ref, misc_ref, kin_ref, *rest, topk, pages_per_step):
    page_refs = rest[:pages_per_step]
    keep_ref, key_ref = rest[pages_per_step], rest[pages_per_step + 1]
    g = pl.program_id(1)
    ng = pl.num_programs(1)
    nq = qi_ref.shape[1]
    page = page_refs[0].shape[1]
    n_pages = key_ref.shape[0] - 1
    past_len = n_pages * page

    qi = qi_ref[0]
    qrows = jnp.concatenate([qi[:, hh * IDX_DIM:(hh + 1) * IDX_DIM] for hh in range(IDX_HEADS)], axis=0)
    qhi, qlo = _split(qrows)
    iw = misc_ref[0][:, :IDX_HEADS]

    def scores(kf):
        khi, klo = _split(kf)
        dots = _dot_nt(qhi, khi) + _dot_nt(qhi, klo) + _dot_nt(qlo, khi) + _dot_nt(qlo, klo)
        dots = jnp.maximum(dots, 0.0)
        sc = jnp.zeros((nq, kf.shape[0]), F32)
        for hh in range(IDX_HEADS):
            sc = sc + dots[hh * nq:(hh + 1) * nq, :] * iw[:, hh:hh + 1]
        return sc

    for pp in range(pages_per_step):
        key_ref[g * pages_per_step + pp] = _float_key(scores(page_refs[pp][0]))

    @pl.when(g == ng - 1)
    def _():
        kin = jnp.concatenate([kin_ref[0], jnp.zeros((page - nq, IDX_DIM), F32)], axis=0)
        qrow2 = lax.broadcasted_iota(I32, (nq, page), 0)
        col2 = lax.broadcasted_iota(I32, (nq, page), 1)
        key_ref[n_pages] = jnp.where(col2 <= qrow2, _float_key(scores(kin)), INT_MIN)

        keys = key_ref[...]
        shape = keys.shape
        col = lax.broadcasted_iota(I32, shape, 0) * page + lax.broadcasted_iota(I32, shape, 2)
        qrow = lax.broadcasted_iota(I32, shape, 1)

        def count(pred):
            return jnp.sum(jnp.sum(jnp.where(pred, 1, 0).astype(I32), axis=0), axis=1, keepdims=True)

        def bit_body(t, ans):
            cand = ans | jnp.left_shift(jnp.int32(1), 31 - t)
            return jnp.where(count(keys >= (cand ^ INT_MIN)) >= topk, cand, ans)

        thr = lax.fori_loop(0, 32, bit_body, jnp.zeros((nq, 1), I32)) ^ INT_MIN
        need = topk - count(keys > thr)
        eq = keys == thr
        idx_bits = int(math.ceil(math.log2(past_len + page)))

        def idx_body(t, xs):
            cand = xs | jnp.left_shift(jnp.int32(1), idx_bits - 1 - t)
            return jnp.where(count(eq & (col < cand)) < need, cand, xs)

        cut = lax.fori_loop(0, idx_bits, idx_body, jnp.zeros((nq, 1), I32))
        keep = ((keys > thr) | (eq & (col <= cut))) & (col <= past_len + qrow)
        keep_ref[0] = jnp.where(keep, 1.0, 0.0)


def _sample_index(page_table, qi, misc, ki_new, cache_kidx, *, topk, pages_per_step):
    b, nq, _ = qi.shape
    n_pages = page_table.shape[1]
    page = cache_kidx.shape[1]
    ng = n_pages // pages_per_step

    def page_spec(pp):
        return pl.BlockSpec((1, page, IDX_DIM),
                            lambda i, g, pt: (pt[i * n_pages + g * pages_per_step + pp], 0, 0))

    seq = lambda c: pl.BlockSpec((1, nq, c), lambda i, g, pt: (i, 0, 0))
    grid_spec = pltpu.PrefetchScalarGridSpec(
        num_scalar_prefetch=1,
        grid=(b, ng),
        in_specs=[seq(512), seq(LANES), seq(IDX_DIM)] + [page_spec(pp) for pp in range(pages_per_step)],
        out_specs=pl.BlockSpec((1, n_pages + 1, nq, page), lambda i, g, pt: (i, 0, 0, 0)),
        scratch_shapes=[pltpu.VMEM((n_pages + 1, nq, page), I32)],
    )
    return pl.pallas_call(
        functools.partial(_sample_index_kernel, topk=topk, pages_per_step=pages_per_step),
        grid_spec=grid_spec,
        out_shape=jax.ShapeDtypeStruct((b, n_pages + 1, nq, page), F32),
        compiler_params=_params(),
        name="sample_index",
    )(page_table.reshape(-1), qi, misc, ki_new, *([cache_kidx] * pages_per_step))


def _sample_attn_kernel(pt_ref, q_ref, keep_ref, keepn_ref, kn_ref, vn_ref, *rest, pages_per_step):
    k_refs = rest[:pages_per_step]
    v_refs = rest[pages_per_step:2 * pages_per_step]
    o_ref, m_ref, l_ref, acc_ref = rest[2 * pages_per_step:]
    g = pl.program_id(1)
    ng = pl.num_programs(1)
    nq = q_ref.shape[1]
    page = k_refs[0].shape[2]
    nr = N_HEADS * nq

    def head_rows(t):
        return jnp.concatenate([t[:, hh * HEAD_DIM:(hh + 1) * HEAD_DIM] for hh in range(N_HEADS)], axis=0)

    qrows = head_rows(q_ref[0]).astype(BF16)

    @pl.when(g == 0)
    def _():
        m_ref[...] = jnp.full(m_ref.shape, MASK_BIAS, F32)
        l_ref[...] = jnp.zeros(l_ref.shape, F32)
        acc_ref[...] = jnp.zeros(acc_ref.shape, F32)

    def update(kr, vr, keep, lane_head):
        n = kr.shape[0]
        rhead = lax.broadcasted_iota(I32, (nr, n), 0) // nq
        ok = (jnp.concatenate([keep] * N_HEADS, axis=0) > 0.5) & (rhead == lane_head)
        s = jnp.where(ok, _dot_nt(qrows, kr), MASK_BIAS)
        m_old = m_ref[...]
        m_new = jnp.maximum(m_old, jnp.max(s, axis=1, keepdims=True))
        p = jnp.exp(s - m_new)
        corr = jnp.exp(m_old - m_new)
        l_ref[...] = l_ref[...] * corr + jnp.sum(p, axis=1, keepdims=True)
        acc_ref[...] = acc_ref[...] * corr + _bdot(p.astype(BF16), vr)
        m_ref[...] = m_new

    n = page * N_HEADS
    expand = jnp.where(lax.broadcasted_iota(I32, (page, n), 1) // N_HEADS
                       == lax.broadcasted_iota(I32, (page, n), 0), 1.0, 0.0).astype(BF16)
    page_lane_head = lax.broadcasted_iota(I32, (nr, n), 1) % N_HEADS
    for pp in range(pages_per_step):
        keep = _bdot(keep_ref[0, pp].astype(BF16), expand)
        update(k_refs[pp][0, 0].reshape(n, HEAD_DIM).astype(BF16),
               v_refs[pp][0, 0].reshape(n, HEAD_DIM).astype(BF16), keep, page_lane_head)

    @pl.when(g == ng - 1)
    def _():
        nn = N_HEADS * nq
        spread = jnp.where((lax.broadcasted_iota(I32, (page, nn), 1) % nq)
                           == lax.broadcasted_iota(I32, (page, nn), 0), 1.0, 0.0).astype(BF16)
        keep = _bdot(keepn_ref[0, 0].astype(BF16), spread)
        update(head_rows(kn_ref[0]).astype(BF16), head_rows(vn_ref[0]).astype(BF16), keep,
               lax.broadcasted_iota(I32, (nr, nn), 1) // nq)
        o = acc_ref[...] / l_ref[...]
        o_ref[0] = jnp.concatenate([o[hh * nq:(hh + 1) * nq, :] for hh in range(N_HEADS)], axis=1)


def _sample_attention(page_table, q, keep, k_new, v_new, cache_k, cache_v, layer, *, pages_per_step):
    b, nq, _ = q.shape
    n_pages = page_table.shape[1]
    page = cache_k.shape[2]
    ng = n_pages // pages_per_step

    def page_spec(pp):
        return pl.BlockSpec((1, 1, page, N_HEADS, HEAD_DIM),
                            lambda i, g, pt: (layer, pt[i * n_pages + g * pages_per_step + pp], 0, 0, 0))

    seq = lambda c: pl.BlockSpec((1, nq, c), lambda i, g, pt: (i, 0, 0))
    grid_spec = pltpu.PrefetchScalarGridSpec(
        num_scalar_prefetch=1,
        grid=(b, ng),
        in_specs=[seq(D_ATTN),
                  pl.BlockSpec((1, pages_per_step, nq, page), lambda i, g, pt: (i, g, 0, 0)),
                  pl.BlockSpec((1, 1, nq, page), lambda i, g, pt: (i, n_pages, 0, 0)),
                  seq(D_ATTN), seq(D_ATTN)] + [page_spec(pp) for pp in range(pages_per_step)] * 2,
        out_specs=seq(D_ATTN),
        scratch_shapes=[pltpu.VMEM((N_HEADS * nq, 1), F32), pltpu.VMEM((N_HEADS * nq, 1), F32),
                        pltpu.VMEM((N_HEADS * nq, HEAD_DIM), F32)],
    )
    return pl.pallas_call(
        functools.partial(_sample_attn_kernel, pages_per_step=pages_per_step),
        grid_spec=grid_spec,
        out_shape=jax.ShapeDtypeStruct((b, nq, D_ATTN), F32),
        compiler_params=_params(),
        name="sample_attention",
    )(page_table.reshape(-1), q, keep, keep, k_new, v_new,
      *([cache_k] * pages_per_step), *([cache_v] * pages_per_step))


def _rope_tables(pos):
    half = HEAD_DIM // 2
    freq = ROPE_THETA ** (-jnp.arange(half, dtype=F32) / half)
    ang = pos.astype(F32)[:, None] * freq[None, :]
    cos, sin = jnp.cos(ang), jnp.sin(ang)
    cos = jnp.tile(jnp.concatenate([cos, cos], axis=-1), (1, LANES // HEAD_DIM))
    sin = jnp.tile(jnp.concatenate([-sin, sin], axis=-1), (1, LANES // HEAD_DIM))
    pos1 = jnp.broadcast_to((pos + 1).astype(F32)[:, None], (pos.shape[0], LANES))
    return cos, sin, pos1


def _pack_w_in(w):
    d = w.shape[0]
    o_q, o_k, o_v, o_qi = D_POOL, D_POOL + D_ATTN, D_POOL + 2 * D_ATTN, D_POOL + 3 * D_ATTN
    o_ki = o_qi + IDX_HEADS * IDX_DIM
    o_iw = o_ki + IDX_DIM
    o_g = o_iw + IDX_HEADS
    parts = [w[:, :o_q], w[:, o_q:o_k] * (HEAD_DIM ** -0.5), w[:, o_k:o_qi],
             w[:, o_qi:o_ki], jnp.tile(w[:, o_ki:o_iw], (1, 4)),
             jnp.pad(w[:, o_iw:o_g], ((0, 0), (0, LANES - IDX_HEADS))), w[:, o_g:]]
    packed = jnp.concatenate(parts, axis=1)
    assert packed.shape == (d, C_END)
    return packed.astype(BF16)


def kernel(x_prompt, x_sample, cache_k, cache_v, cache_kidx, state_pool, state_conv, page_table, norm1_g,
           w_in, b_gate, pool_mix_w, pool_scale, w_pool_o, w_attn_o, w_out, norm2_g, w_up, conv_w, conv_b,
           w_down, normf_g):
    depth = w_in.shape[0]
    b, l, d = x_prompt.shape
    db, dl, _ = x_sample.shape
    n_pages = page_table.shape[1]
    page = cache_k.shape[2]
    past = n_pages * page
    c2 = w_up.shape[2]
    assert d == 1024 and dl % SUBLANES == 0 and l % KEY_CHUNK == 0

    cos_p, sin_p, pos1_p = _rope_tables(jnp.arange(l))
    cos_s, sin_s, pos1_s = _rope_tables(past + jnp.arange(dl))
    topk_p = min(INDEX_TOPK, l // 4)
    topk_s = min(INDEX_TOPK, (past + dl) // 4)
    sample_bt = min(db, 256 // dl)

    xp, xs = x_prompt, x_sample
    outs_p = [[] for _ in range(5)]
    outs_s = [[] for _ in range(5)]
    for li in range(depth):
        w_all = _pack_w_in(w_in[li])
        g1 = norm1_g[li][None]
        bg = b_gate[li][None]
        mixw = pool_mix_w[li].astype(BF16)
        pscale = pool_scale[li][None]
        wpo = w_pool_o[li].astype(BF16)
        wao = w_attn_o[li].astype(BF16)
        wout = w_out[li].astype(BF16)
        g2 = norm2_g[li][None]
        wup = w_up[li].astype(BF16)
        wdn = w_down[li].astype(BF16)
        cw = conv_w[li]
        cb = conv_b[li][None]
        gf = jnp.ones((1, d), F32)

        (pa, gb, q, k, kb, v, vt, qi, misc, ki, ki4, ptail) = _mixer_in(
            xp, jnp.zeros((b, POOL_PAD, D_POOL), F32), cos_p, sin_p, pos1_p, g1, w_all, bg, mixw, pscale, wpo,
            bt=1, lt=KEY_CHUNK, emit_vt=True, narrow=BF16)
        attn = _prompt_attention(qi, misc, q, ki4, kb, vt, topk=topk_p)
        last = li == depth - 1
        x1p, ctail = _output_stage_layer(xp, pa, gb, attn, jnp.zeros((b, CONV_PAD, c2), F32), wao, wout, g2,
                                         wup, cw, cb, wdn, normf_g[None] if last else gf, last,
                                         bt=1, lt=256)
        outs_p[0].append(k.reshape(b, l, N_HEADS, HEAD_DIM))
        outs_p[1].append(v.reshape(b, l, N_HEADS, HEAD_DIM))
        outs_p[2].append(ki)
        outs_p[3].append(ptail[:, 1:])
        outs_p[4].append(ctail[:, CONV_PAD - 2:])
        xp = x1p

        pre_pool = jnp.concatenate([jnp.zeros((db, 1, D_POOL), F32), state_pool[li]], axis=1)
        (pa, gb, q, k, _, v, _, qi, misc, ki, _, ptail) = _mixer_in(
            xs, pre_pool, cos_s, sin_s, pos1_s, g1, w_all, bg, mixw, pscale, wpo,
            bt=sample_bt, lt=dl, emit_vt=False, narrow=F32)
        keep = _sample_index(page_table, qi, misc, ki, cache_kidx[li], topk=topk_s, pages_per_step=8)
        attn = _sample_attention(page_table, q, keep, k, v, cache_k, cache_v, li, pages_per_step=4)
        pre_conv = jnp.concatenate([jnp.zeros((db, CONV_PAD - 2, c2), F32), state_conv[li]], axis=1)
        x1s, ctail = _output_stage_layer(xs, pa, gb, attn, pre_conv, wao, wout, g2, wup, cw, cb, wdn,
                                         normf_g[None] if last else gf, last, bt=sample_bt, lt=dl)
        outs_s[0].append(k.reshape(db, dl, N_HEADS, HEAD_DIM))
        outs_s[1].append(v.reshape(db, dl, N_HEADS, HEAD_DIM))
        outs_s[2].append(ki)
        outs_s[3].append(ptail[:, 1:])
        outs_s[4].append(ctail[:, CONV_PAD - 2:])
        xs = x1s

    return (xp, xs, *[jnp.stack(o) for o in outs_p], *[jnp.stack(o) for o in outs_s])


def _output_stage_layer(x, pa, gb, attn, prefix8, wao, wout, g2, wup, cw, cb, wdn, gf, last, *, bt, lt):
    assert last, "only a single-layer stack is supported: the final norm is fused into the output stage"
    return _output_stage(x, pa, gb, attn, prefix8, wao, wout, g2, wup, cw, cb, wdn, gf,
                         bt=bt, lt=lt, ff_chunk=256)
```

```python
import functools
import math

import jax
import jax.numpy as jnp
from jax import lax
from jax.experimental import pallas as pl
from jax.experimental.pallas import tpu as pltpu

F32 = jnp.float32
BF16 = jnp.bfloat16
I32 = jnp.int32

LANES = 128
SUBLANES = 8
VMEM_LIMIT_BYTES = 56 * 1024 * 1024

N_HEADS = 8
HEAD_DIM = 64
D_ATTN = N_HEADS * HEAD_DIM
IDX_HEADS = 8
IDX_DIM = 64
D_POOL = 512
POOL_WINDOWS = (2, 4, 8, 16)
POOL_PAD = 16
CONV_PAD = 8
INDEX_TOPK = 256
Q_BLOCK = 128
KEY_CHUNK = 256
ROPE_THETA = 10000.0
RMS_EPS = 1e-6
MASK_BIAS = -1e30
INT_MIN = -(2 ** 31)

C_P, C_Q, C_K, C_V, C_QI = 0, 512, 1024, 1536, 2048
C_KI4 = 2560
C_IW = 2816
C_GA = 2944
C_GB = 3968
C_END = 4992


def _params():
    return pltpu.CompilerParams(dimension_semantics=("arbitrary", "arbitrary"),
                                vmem_limit_bytes=VMEM_LIMIT_BYTES)


def _rms(x, g):
    return x * lax.rsqrt(jnp.mean(x * x, axis=-1, keepdims=True) + RMS_EPS) * g


def _rope_blocks(z, cos, sin, first_half):
    outs = []
    for c in range(z.shape[1] // LANES):
        zb = z[:, c * LANES:(c + 1) * LANES]
        sw = jnp.where(first_half, pltpu.roll(zb, LANES - 32, 1), pltpu.roll(zb, 32, 1))
        outs.append(zb * cos + sw * sin)
    return outs[0] if len(outs) == 1 else jnp.concatenate(outs, axis=1)


def _bdot(a, b):
    return jnp.dot(a, b, preferred_element_type=F32)


def _dot_nt(a, b):
    return lax.dot_general(a, b, (((1,), (1,)), ((), ())), preferred_element_type=F32)


def _split(x):
    hi = x.astype(BF16)
    return hi, (x - hi.astype(F32)).astype(BF16)


def _mixer_in_kernel(x_ref, pre_ref, cos_ref, sin_ref, pos1_ref, g1_ref, w_ref, bg_ref, mixw_ref,
                     pscale_ref, wpo_ref, pa_ref, gb_ref, q_ref, qi_ref, misc_ref, tail_ref, *rest, prompt):
    carry_ref = rest[-1]
    j = pl.program_id(1)
    bt, lt, d = x_ref.shape
    m = bt * lt
    narrow = pa_ref.dtype

    x = x_ref[...].reshape(m, d)
    h = _rms(x, g1_ref[...]).astype(BF16)

    def rows(ref):
        t = ref[...]
        if bt == 1:
            return t
        return jnp.broadcast_to(t[None], (bt, lt, LANES)).reshape(m, LANES)

    cos, sin, pos1 = rows(cos_ref), rows(sin_ref), rows(pos1_ref)
    lane = lax.broadcasted_iota(I32, (m, LANES), 1)
    first_half = (lane % 64) < 32

    def proj(c0, c1):
        return _bdot(h, w_ref[:, c0:c1])

    p = proj(C_P, C_Q)

    @pl.when(j == 0)
    def _():
        carry_ref[...] = pre_ref[...]

    ext = jnp.concatenate([carry_ref[...], p.reshape(bt, lt, D_POOL)], axis=1)
    tail = ext[:, lt:, :]
    carry_ref[...] = tail
    tail_ref[...] = tail
    e2 = ext.reshape(bt * (POOL_PAD + lt), D_POOL)
    pools = []
    for g, w in enumerate(POOL_WINDOWS):
        s = e2[:, g * LANES:(g + 1) * LANES]
        sh = 1
        while sh < w:
            s = s + pltpu.roll(s, sh, 0)
            sh *= 2
        wsum = s.reshape(bt, POOL_PAD + lt, LANES)[:, POOL_PAD:, :].reshape(m, LANES)
        dlt = wsum / jnp.minimum(pos1, float(w)) - p[:, g * LANES:(g + 1) * LANES]
        pools.append(_bdot(dlt.astype(BF16), mixw_ref[g]))
    pool = jnp.concatenate(pools, axis=1) * pscale_ref[...]
    a = _bdot(pool.astype(BF16), wpo_ref[...])

    bg = bg_ref[...]
    ga = jax.nn.sigmoid(proj(C_GA, C_GB) + bg[:, :d])
    pa_ref[...] = (ga * a).astype(narrow).reshape(bt, lt, d)
    gb = jax.nn.sigmoid(proj(C_GB, C_END) + bg[:, d:])
    gb_ref[...] = gb.astype(narrow).reshape(bt, lt, d)

    q = _rope_blocks(proj(C_Q, C_K), cos, sin, first_half)
    q_ref[...] = q.astype(narrow).reshape(bt, lt, D_ATTN)
    qi = _rope_blocks(proj(C_QI, C_KI4), cos, sin, first_half)
    qi_ref[...] = qi.reshape(bt, lt, 512)
    misc_ref[...] = (proj(C_IW, C_GA) * (IDX_HEADS ** -0.5 * IDX_DIM ** -0.5)).reshape(bt, lt, LANES)
    k = _rope_blocks(proj(C_K, C_V), cos, sin, first_half)
    v = proj(C_V, C_QI)
    ki4 = _rope_blocks(proj(C_KI4, C_IW), cos, sin, first_half)
    if prompt:
        kt_ref, kb_ref, vt_ref, vtb_ref, kit_ref, ki4_ref = rest[:-1]
        kt_ref[0] = k.T
        kb_ref[...] = k.astype(BF16).reshape(bt, lt, D_ATTN)
        vt = v.T
        vt_ref[0] = vt
        vtb_ref[0, 0] = vt.astype(BF16)
        kit_ref[0] = ki4[:, :LANES].T[:IDX_DIM, :]
        hi, lo = _split(ki4)
        ki4_ref[...] = jnp.concatenate([hi[:, :LANES], lo[:, LANES:]], axis=1).reshape(bt, lt, 256)
    else:
        k_ref, v_ref, ki_ref = rest[:-1]
        k_ref[...] = k.reshape(bt, lt, D_ATTN)
        v_ref[...] = v.reshape(bt, lt, D_ATTN)
        ki_ref[...] = ki4[:, :IDX_DIM].reshape(bt, lt, IDX_DIM)


def _mixer_in(x, prefix16, cos, sin, pos1, g1, w_all, bg, mixw, pscale, wpo, *, bt, lt, prompt):
    b, l, d = x.shape
    nb, nj = b // bt, l // lt
    narrow = BF16 if prompt else F32
    tok = lambda c: pl.BlockSpec((bt, lt, c), lambda i, j: (i, j, 0))
    full = lambda a: pl.BlockSpec(a.shape, lambda i, j: (0,) * a.ndim)
    tab = pl.BlockSpec((lt, LANES), lambda i, j: (j, 0))
    pre = pl.BlockSpec((bt, POOL_PAD, D_POOL), lambda i, j: (i, 0, 0))
    sds = lambda c, dt: jax.ShapeDtypeStruct((b, l, c), dt)
    out_shape = [sds(d, narrow), sds(d, narrow), sds(D_ATTN, narrow), sds(512, F32), sds(LANES, F32),
                 jax.ShapeDtypeStruct((b, POOL_PAD, D_POOL), F32)]
    out_specs = [tok(d), tok(d), tok(D_ATTN), tok(512), tok(LANES), pre]
    if prompt:
        assert bt == 1
        tr = lambda c: pl.BlockSpec((1, c, lt), lambda i, j: (i, 0, j))
        trs = lambda c: jax.ShapeDtypeStruct((b, c, l), F32)
        out_shape += [trs(D_ATTN), sds(D_ATTN, BF16), trs(D_ATTN),
                      jax.ShapeDtypeStruct((b, nj, D_ATTN, lt), BF16), trs(IDX_DIM), sds(256, BF16)]
        out_specs += [tr(D_ATTN), tok(D_ATTN), tr(D_ATTN),
                      pl.BlockSpec((1, 1, D_ATTN, lt), lambda i, j: (i, j, 0, 0)), tr(IDX_DIM), tok(256)]
    else:
        out_shape += [sds(D_ATTN, F32), sds(D_ATTN, F32), sds(IDX_DIM, F32)]
        out_specs += [tok(D_ATTN), tok(D_ATTN), tok(IDX_DIM)]
    return pl.pallas_call(
        functools.partial(_mixer_in_kernel, prompt=prompt),
        grid=(nb, nj),
        in_specs=[tok(d), pre, tab, tab, tab,
                  full(g1), full(w_all), full(bg), full(mixw), full(pscale), full(wpo)],
        out_specs=tuple(out_specs),
        out_shape=tuple(out_shape),
        scratch_shapes=[pltpu.VMEM((bt, POOL_PAD, D_POOL), F32)],
        compiler_params=_params(),
        name="mixer_in",
    )(x, prefix16, cos, sin, pos1, g1, w_all, bg, mixw, pscale, wpo)


def _float_key(s):
    b = lax.bitcast_convert_type(s, I32)
    k = b ^ ((b >> 31) & 0x7FFFFFFF)
    return jnp.where(b == INT_MIN, 0, k)


def _count(key_ref, n_chunks, chunk, pred):
    width = key_ref.shape[1]

    def body(c, acc):
        r0 = pl.multiple_of(c * chunk, chunk)
        kc = key_ref[pl.ds(r0, chunk), :]
        hit = jnp.where(pred(kc, r0), 1, 0).astype(I32)
        return acc + jnp.sum(hit.reshape(chunk // SUBLANES, SUBLANES, width), axis=0)

    acc = lax.fori_loop(0, n_chunks, body, jnp.zeros((SUBLANES, width), I32))
    return jnp.sum(acc, axis=0, keepdims=True)


def _topk_select(key_ref, n_chunks, chunk, topk, idx_bits):
    width = key_ref.shape[1]

    def bit_body(t, ans):
        bit = jnp.left_shift(jnp.int32(1), 31 - t)
        cand = ans | bit
        ck = cand ^ INT_MIN
        cnt = _count(key_ref, n_chunks, chunk, lambda kc, r0: kc >= ck)
        return jnp.where(cnt >= topk, cand, ans)

    ans = lax.fori_loop(0, 32, bit_body, jnp.zeros((1, width), I32))
    thr = ans ^ INT_MIN
    n_gt = _count(key_ref, n_chunks, chunk, lambda kc, r0: kc > thr)
    n_eq = _count(key_ref, n_chunks, chunk, lambda kc, r0: kc == thr)
    need = topk - n_gt
    big = jnp.full((1, width), 2 ** idx_bits, I32)

    def tie_search():
        def idx_body(t, xs):
            cand = xs | jnp.left_shift(jnp.int32(1), idx_bits - 1 - t)

            def pred(kc, r0):
                row = r0 + lax.broadcasted_iota(I32, kc.shape, 0)
                return (kc == thr) & (row < cand)

            cnt = _count(key_ref, n_chunks, chunk, pred)
            return jnp.where(cnt < need, cand, xs)

        return lax.fori_loop(0, idx_bits, idx_body, jnp.zeros((1, width), I32))

    has_tie = jnp.max(n_eq - need) > 0
    cut = lax.cond(has_tie, tie_search, lambda: big)
    return thr, cut


def _prompt_attn_kernel(qi_ref, misc_ref, q_ref, ki4_ref, kb_ref, vt_ref, o_ref,
                        key_ref, bias_ref, rhs_ref, qbd_ref, acc_ref, *, topk):
    i = pl.program_id(1)
    qb = q_ref.shape[1]
    lk = ki4_ref.shape[1]
    chunk = vt_ref.shape[3]
    n_pairs = N_HEADS // 2
    n_chunks = (i * qb + qb + chunk - 1) // chunk
    q_pos = i * qb + lax.broadcasted_iota(I32, (1, qb), 1)

    qhi, qlo = _split(qi_ref[0].T)
    iwt = misc_ref[0].T[:IDX_HEADS, :]

    def head_rhs(hh):
        a, b = qhi[hh * 64:(hh + 1) * 64], qlo[hh * 64:(hh + 1) * 64]
        return jnp.concatenate([a, b, a, b], axis=0)

    for jp in range(n_pairs):
        rhs_ref[jp] = jnp.concatenate([head_rhs(2 * jp), head_rhs(2 * jp + 1)], axis=1)

    def score_body(c, carry):
        r0 = pl.multiple_of(c * chunk, chunk)
        kc = ki4_ref[0, pl.ds(r0, chunk), :]
        sc = jnp.zeros((chunk, qb), F32)
        for jp in range(n_pairs):
            dots = jnp.maximum(_bdot(kc, rhs_ref[jp]), 0.0)
            sc = sc + dots[:, :qb] * iwt[2 * jp:2 * jp + 1, :] + dots[:, qb:] * iwt[2 * jp + 1:2 * jp + 2, :]
        row = r0 + lax.broadcasted_iota(I32, (chunk, qb), 0)
        key_ref[pl.ds(r0, chunk), :] = jnp.where(row <= q_pos, _float_key(sc), INT_MIN)
        return carry

    lax.fori_loop(0, n_chunks, score_body, 0)

    def searched():
        return _topk_select(key_ref, n_chunks, chunk, topk, int(math.log2(lk)))

    def everything():
        return (jnp.full((1, qb), INT_MIN, I32), jnp.full((1, qb), lk, I32))

    thr, cut = lax.cond((i * qb + qb) > topk, searched, everything)

    def bias_body(c, carry):
        r0 = pl.multiple_of(c * chunk, chunk)
        kc = key_ref[pl.ds(r0, chunk), :]
        row = r0 + lax.broadcasted_iota(I32, (chunk, qb), 0)
        keep = ((kc > thr) | ((kc == thr) & (row <= cut))) & (row <= q_pos)
        bias_ref[pl.ds(r0, chunk), :] = jnp.where(keep, 0.0, MASK_BIAS)
        return carry

    lax.fori_loop(0, n_chunks, bias_body, 0)

    qt = q_ref[0].astype(F32).T.astype(BF16)
    zero = jnp.zeros((HEAD_DIM, qb), BF16)
    for jp in range(n_pairs):
        h0, h1 = 2 * jp, 2 * jp + 1
        qbd_ref[jp] = jnp.concatenate(
            [jnp.concatenate([qt[h0 * 64:(h0 + 1) * 64], zero], axis=1),
             jnp.concatenate([zero, qt[h1 * 64:(h1 + 1) * 64]], axis=1)], axis=0)
    acc_ref[...] = jnp.zeros(acc_ref.shape, F32)

    def attn_body(c, carry):
        r0 = pl.multiple_of(c * chunk, chunk)
        bias = bias_ref[pl.ds(r0, chunk), :]
        bias2 = jnp.concatenate([bias, bias], axis=1)
        s_all = [_bdot(kb_ref[0, pl.ds(r0, chunk), jp * LANES:(jp + 1) * LANES], qbd_ref[jp])
                 for jp in range(n_pairs)]
        new, ps, corrs = [], [], []
        for jp in range(n_pairs):
            m_run, l_run = carry[jp]
            s = s_all[jp] + bias2
            m_new = jnp.maximum(m_run, jnp.max(s, axis=0, keepdims=True))
            p = jnp.exp(s - m_new)
            corr = jnp.exp(m_run - m_new)
            new.append((m_new, l_run * corr + jnp.sum(p, axis=0, keepdims=True)))
            ps.append(p.astype(BF16))
            corrs.append(corr)
        pv = [_bdot(vt_ref[0, c, jp * LANES:(jp + 1) * LANES, :], ps[jp]) for jp in range(n_pairs)]
        for jp in range(n_pairs):
            acc_ref[jp] = acc_ref[jp] * corrs[jp] + pv[jp]
        return tuple(new)

    init = tuple((jnp.full((1, 2 * qb), MASK_BIAS, F32), jnp.zeros((1, 2 * qb), F32))
                 for _ in range(n_pairs))
    fin = lax.fori_loop(0, n_chunks, attn_body, init)
    outs = []
    for jp in range(n_pairs):
        o = acc_ref[jp] / fin[jp][1]
        outs.append(o[:HEAD_DIM, :qb])
        outs.append(o[HEAD_DIM:, qb:])
    o_ref[0] = jnp.concatenate(outs, axis=0).T.astype(o_ref.dtype)


def _prompt_attention(qi, misc, q, ki4, kb, vt, *, topk):
    b, l, _ = q.shape
    nq = l // Q_BLOCK
    blk = lambda c: pl.BlockSpec((1, Q_BLOCK, c), lambda i, j: (i, j, 0))
    seq = lambda c: pl.BlockSpec((1, l, c), lambda i, j: (i, 0, 0))
    n_pairs = N_HEADS // 2
    return pl.pallas_call(
        functools.partial(_prompt_attn_kernel, topk=topk),
        grid=(b, nq),
        in_specs=[blk(512), blk(LANES), blk(D_ATTN), seq(256), seq(D_ATTN),
                  pl.BlockSpec((1,) + vt.shape[1:], lambda i, j: (i, 0, 0, 0))],
        out_specs=blk(D_ATTN),
        out_shape=jax.ShapeDtypeStruct((b, l, D_ATTN), BF16),
        scratch_shapes=[pltpu.VMEM((l, Q_BLOCK), I32), pltpu.VMEM((l, Q_BLOCK), F32),
                        pltpu.VMEM((n_pairs, 4 * IDX_DIM, 2 * Q_BLOCK), BF16),
                        pltpu.VMEM((n_pairs, 2 * HEAD_DIM, 2 * Q_BLOCK), BF16),
                        pltpu.VMEM((n_pairs, 2 * HEAD_DIM, 2 * Q_BLOCK), F32)],
        compiler_params=_params(),
        name="prompt_attention",
    )(qi, misc, q, ki4, kb, vt)


def _gelu_tanh(x):
    return 0.5 * x * (1.0 + jnp.tanh(math.sqrt(2.0 / math.pi) * (x + 0.044715 * (x * x * x))))


def _output_kernel(x_ref, pa_ref, gb_ref, at_ref, pre_ref, wao_ref, wout_ref, g2_ref, wup_ref, cw_ref,
                   cb_ref, wdn_ref, gf_ref, y_ref, tail_ref, carry_ref, *, ff_chunk):
    j = pl.program_id(1)
    bt, lt, d = x_ref.shape
    m = bt * lt
    dff = wdn_ref.shape[0]

    x = x_ref[...].reshape(m, d)
    at = at_ref[...].reshape(m, D_ATTN).astype(BF16)
    mrg = (pa_ref[...].reshape(m, d).astype(F32)
           + gb_ref[...].reshape(m, d).astype(F32) * _bdot(at, wao_ref[...]))
    x1 = x + _bdot(mrg.astype(BF16), wout_ref[...])
    h2 = _rms(x1, g2_ref[...]).astype(BF16)

    @pl.when(j == 0)
    def _():
        carry_ref[...] = pre_ref[...]

    def conv(c0):
        u = _bdot(h2, wup_ref[:, c0:c0 + ff_chunk])
        ext = jnp.concatenate([carry_ref[:, :, c0:c0 + ff_chunk], u.reshape(bt, lt, ff_chunk)], axis=1)
        carry_ref[:, :, c0:c0 + ff_chunk] = ext[:, lt:, :]
        e2 = ext.reshape(bt * (CONV_PAD + lt), ff_chunk)

        def shifted(e):
            return e.reshape(bt, CONV_PAD + lt, ff_chunk)[:, CONV_PAD:, :].reshape(m, ff_chunk)

        cw = cw_ref[:, c0:c0 + ff_chunk]
        return (cb_ref[:, c0:c0 + ff_chunk] + cw[0:1] * shifted(pltpu.roll(e2, 2, 0))
                + cw[1:2] * shifted(pltpu.roll(e2, 1, 0)) + cw[2:3] * u)

    f = jnp.zeros((m, d), F32)
    for cc in range(dff // ff_chunk):
        gate = conv(cc * ff_chunk)
        val = conv(dff + cc * ff_chunk)
        act = (_gelu_tanh(gate) * val).astype(BF16)
        f = f + _bdot(act, wdn_ref[cc * ff_chunk:(cc + 1) * ff_chunk, :])
    tail_ref[...] = carry_ref[...]
    y_ref[...] = _rms(x1 + f, gf_ref[...]).reshape(bt, lt, d)


def _output_stage(x, pa, gb, attn, prefix8, wao, wout, g2, wup, cw, cb, wdn, gf, *, bt, lt, ff_chunk):
    b, l, d = x.shape
    nb, nj = b // bt, l // lt
    c2 = wup.shape[1]
    tok = lambda c: pl.BlockSpec((bt, lt, c), lambda i, j: (i, j, 0))
    full = lambda a: pl.BlockSpec(a.shape, lambda i, j: (0,) * a.ndim)
    pre = pl.BlockSpec((bt, CONV_PAD, c2), lambda i, j: (i, 0, 0))
    return pl.pallas_call(
        functools.partial(_output_kernel, ff_chunk=ff_chunk),
        grid=(nb, nj),
        in_specs=[tok(d), tok(d), tok(d), tok(D_ATTN), pre, full(wao), full(wout), full(g2), full(wup),
                  full(cw), full(cb), full(wdn), full(gf)],
        out_specs=(tok(d), pre),
        out_shape=(jax.ShapeDtypeStruct((b, l, d), F32), jax.ShapeDtypeStruct((b, CONV_PAD, c2), F32)),
        scratch_shapes=[pltpu.VMEM((bt, CONV_PAD, c2), F32)],
        compiler_params=_params(),
        name="output_stage",
    )(x, pa, gb, attn, prefix8, wao, wout, g2, wup, cw, cb, wdn, gf)


def _sample_index_kernel(pt_ref, qi_ref, misc_ref, kin_ref, *rest, topk, pages_per_step):
    page_refs = rest[:pages_per_step]
    keep_ref, key_ref = rest[pages_per_step], rest[pages_per_step + 1]
    g = pl.program_id(1)
    ng = pl.num_programs(1)
    nq = qi_ref.shape[1]
    page = page_refs[0].shape[3]
    n_pages = key_ref.shape[0] - 1
    past_len = n_pages * page

    qi = qi_ref[0]
    qrows = jnp.concatenate([qi[:, hh * IDX_DIM:(hh + 1) * IDX_DIM] for hh in range(IDX_HEADS)], axis=0)
    qhi, qlo = _split(qrows)
    iw = misc_ref[0][:, :IDX_HEADS]

    def scores(kt):
        khi, klo = _split(kt)
        dots = _bdot(qhi, khi) + _bdot(qhi, klo) + _bdot(qlo, khi) + _bdot(qlo, klo)
        dots = jnp.maximum(dots, 0.0)
        sc = jnp.zeros((nq, kt.shape[1]), F32)
        for hh in range(IDX_HEADS):
            sc = sc + dots[hh * nq:(hh + 1) * nq, :] * iw[:, hh:hh + 1]
        return sc

    step_keys = _float_key(scores(jnp.concatenate([r[0, 0] for r in page_refs], axis=1)))
    for pp in range(pages_per_step):
        key_ref[g * pages_per_step + pp] = step_keys[:, pp * page:(pp + 1) * page]

    @pl.when(g == ng - 1)
    def _():
        kin = jnp.concatenate([kin_ref[0], jnp.zeros((page - nq, IDX_DIM), F32)], axis=0)
        qrow2 = lax.broadcasted_iota(I32, (nq, page), 0)
        col2 = lax.broadcasted_iota(I32, (nq, page), 1)
        key_ref[n_pages] = jnp.where(col2 <= qrow2, _float_key(scores(kin.T)), INT_MIN)

        keys = key_ref[...]
        shape = keys.shape
        col = lax.broadcasted_iota(I32, shape, 0) * page + lax.broadcasted_iota(I32, shape, 2)
        qrow = lax.broadcasted_iota(I32, shape, 1)

        def count(pred):
            return jnp.sum(jnp.sum(jnp.where(pred, 1, 0).astype(I32), axis=0), axis=1, keepdims=True)

        def bit_body(t, ans):
            cand = ans | jnp.left_shift(jnp.int32(1), 31 - t)
            return jnp.where(count(keys >= (cand ^ INT_MIN)) >= topk, cand, ans)

        thr = lax.fori_loop(0, 32, bit_body, jnp.zeros((nq, 1), I32)) ^ INT_MIN
        need = topk - count(keys > thr)
        eq = keys == thr
        idx_bits = int(math.ceil(math.log2(past_len + page)))

        def tie_search():
            def idx_body(t, xs):
                cand = xs | jnp.left_shift(jnp.int32(1), idx_bits - 1 - t)
                return jnp.where(count(eq & (col < cand)) < need, cand, xs)

            return lax.fori_loop(0, idx_bits, idx_body, jnp.zeros((nq, 1), I32))

        has_tie = jnp.max(count(eq) - need) > 0
        cut = lax.cond(has_tie, tie_search, lambda: jnp.full((nq, 1), 2 ** idx_bits, I32))
        keep = ((keys > thr) | (eq & (col <= cut))) & (col <= past_len + qrow)
        keep_ref[0] = jnp.where(keep, 1.0, 0.0)


def _sample_index(page_table, qi, misc, ki_new, cache_kidx_t, layer, *, topk, pages_per_step):
    b, nq, _ = qi.shape
    n_pages = page_table.shape[1]
    page = cache_kidx_t.shape[3]
    pages_per_step = min(pages_per_step, n_pages)
    ng = n_pages // pages_per_step

    def page_spec(pp):
        return pl.BlockSpec((1, 1, IDX_DIM, page),
                            lambda i, g, pt: (layer, pt[i * n_pages + g * pages_per_step + pp], 0, 0))

    seq = lambda c: pl.BlockSpec((1, nq, c), lambda i, g, pt: (i, 0, 0))
    grid_spec = pltpu.PrefetchScalarGridSpec(
        num_scalar_prefetch=1,
        grid=(b, ng),
        in_specs=[seq(512), seq(LANES), seq(IDX_DIM)] + [page_spec(pp) for pp in range(pages_per_step)],
        out_specs=pl.BlockSpec((1, n_pages + 1, nq, page), lambda i, g, pt: (i, 0, 0, 0)),
        scratch_shapes=[pltpu.VMEM((n_pages + 1, nq, page), I32)],
    )
    return pl.pallas_call(
        functools.partial(_sample_index_kernel, topk=topk, pages_per_step=pages_per_step),
        grid_spec=grid_spec,
        out_shape=jax.ShapeDtypeStruct((b, n_pages + 1, nq, page), F32),
        compiler_params=_params(),
        name="sample_index",
    )(page_table.reshape(-1), qi, misc, ki_new, *([cache_kidx_t] * pages_per_step))


def _sample_attn_kernel(pt_ref, q_ref, keep_ref, keepn_ref, kn_ref, vn_ref, *rest, pages_per_step):
    k_refs = rest[:pages_per_step]
    v_refs = rest[pages_per_step:2 * pages_per_step]
    o_ref, m_ref, l_ref, acc_ref = rest[2 * pages_per_step:]
    g = pl.program_id(1)
    ng = pl.num_programs(1)
    nq = q_ref.shape[1]
    page = k_refs[0].shape[4]

    q = q_ref[0].astype(BF16)
    qh = [q[:, hh * HEAD_DIM:(hh + 1) * HEAD_DIM] for hh in range(N_HEADS)]

    @pl.when(g == 0)
    def _():
        m_ref[...] = jnp.full(m_ref.shape, MASK_BIAS, F32)
        l_ref[...] = jnp.zeros(l_ref.shape, F32)
        acc_ref[...] = jnp.zeros(acc_ref.shape, F32)

    def update(score_h, out_h, keep):
        s = jnp.concatenate([score_h(hh) for hh in range(N_HEADS)], axis=0)
        s = jnp.where(jnp.concatenate([keep] * N_HEADS, axis=0) > 0.5, s, MASK_BIAS)
        m_old = m_ref[...]
        m_new = jnp.maximum(m_old, jnp.max(s, axis=1, keepdims=True))
        p = jnp.exp(s - m_new)
        corr = jnp.exp(m_old - m_new)
        l_ref[...] = l_ref[...] * corr + jnp.sum(p, axis=1, keepdims=True)
        pb = p.astype(BF16)
        pv = jnp.concatenate([out_h(hh, pb[hh * nq:(hh + 1) * nq]) for hh in range(N_HEADS)], axis=0)
        acc_ref[...] = acc_ref[...] * corr + pv
        m_ref[...] = m_new

    def head_cat(refs, hh):
        return jnp.concatenate([r[0, 0, hh] for r in refs], axis=1).astype(BF16)

    update(lambda hh: _bdot(qh[hh], head_cat(k_refs, hh)),
           lambda hh, p: _dot_nt(p, head_cat(v_refs, hh)),
           jnp.concatenate([keep_ref[0, pp] for pp in range(pages_per_step)], axis=1))

    @pl.when(g == ng - 1)
    def _():
        pad = jnp.zeros((page - nq, D_ATTN), F32)
        kn = jnp.concatenate([kn_ref[0], pad], axis=0).astype(BF16)
        vn = jnp.concatenate([vn_ref[0], pad], axis=0).astype(BF16)
        update(lambda hh: _dot_nt(qh[hh], kn[:, hh * HEAD_DIM:(hh + 1) * HEAD_DIM]),
               lambda hh, p: _bdot(p, vn[:, hh * HEAD_DIM:(hh + 1) * HEAD_DIM]), keepn_ref[0, 0])
        o = acc_ref[...] / l_ref[...]
        o_ref[0] = jnp.concatenate([o[hh * nq:(hh + 1) * nq, :] for hh in range(N_HEADS)], axis=1)


def _sample_attention(page_table, q, keep, k_new, v_new, cache_kt, cache_vt, layer, *, pages_per_step):
    b, nq, _ = q.shape
    n_pages = page_table.shape[1]
    page = cache_kt.shape[4]
    pages_per_step = min(pages_per_step, n_pages)
    ng = n_pages // pages_per_step

    def page_spec(pp):
        return pl.BlockSpec((1, 1, N_HEADS, HEAD_DIM, page),
                            lambda i, g, pt: (layer, pt[i * n_pages + g * pages_per_step + pp], 0, 0, 0))

    seq = lambda c: pl.BlockSpec((1, nq, c), lambda i, g, pt: (i, 0, 0))
    grid_spec = pltpu.PrefetchScalarGridSpec(
        num_scalar_prefetch=1,
        grid=(b, ng),
        in_specs=[seq(D_ATTN),
                  pl.BlockSpec((1, pages_per_step, nq, page), lambda i, g, pt: (i, g, 0, 0)),
                  pl.BlockSpec((1, 1, nq, page), lambda i, g, pt: (i, n_pages, 0, 0)),
                  seq(D_ATTN), seq(D_ATTN)] + [page_spec(pp) for pp in range(pages_per_step)] * 2,
        out_specs=seq(D_ATTN),
        scratch_shapes=[pltpu.VMEM((N_HEADS * nq, 1), F32), pltpu.VMEM((N_HEADS * nq, 1), F32),
                        pltpu.VMEM((N_HEADS * nq, HEAD_DIM), F32)],
    )
    return pl.pallas_call(
        functools.partial(_sample_attn_kernel, pages_per_step=pages_per_step),
        grid_spec=grid_spec,
        out_shape=jax.ShapeDtypeStruct((b, nq, D_ATTN), F32),
        compiler_params=_params(),
        name="sample_attention",
    )(page_table.reshape(-1), q, keep, keep, k_new, v_new,
      *([cache_kt] * pages_per_step), *([cache_vt] * pages_per_step))


def _rope_tables(pos):
    half = HEAD_DIM // 2
    freq = ROPE_THETA ** (-jnp.arange(half, dtype=F32) / half)
    ang = pos.astype(F32)[:, None] * freq[None, :]
    cos, sin = jnp.cos(ang), jnp.sin(ang)
    cos = jnp.tile(jnp.concatenate([cos, cos], axis=-1), (1, LANES // HEAD_DIM))
    sin = jnp.tile(jnp.concatenate([-sin, sin], axis=-1), (1, LANES // HEAD_DIM))
    pos1 = jnp.broadcast_to((pos + 1).astype(F32)[:, None], (pos.shape[0], LANES))
    return cos, sin, pos1


def _pack_w_in(w):
    d = w.shape[0]
    o_q, o_k, o_v, o_qi = D_POOL, D_POOL + D_ATTN, D_POOL + 2 * D_ATTN, D_POOL + 3 * D_ATTN
    o_ki = o_qi + IDX_HEADS * IDX_DIM
    o_iw = o_ki + IDX_DIM
    o_g = o_iw + IDX_HEADS
    parts = [w[:, :o_q], w[:, o_q:o_k] * (HEAD_DIM ** -0.5), w[:, o_k:o_qi],
             w[:, o_qi:o_ki], jnp.tile(w[:, o_ki:o_iw], (1, 4)),
             jnp.pad(w[:, o_iw:o_g], ((0, 0), (0, LANES - IDX_HEADS))), w[:, o_g:]]
    packed = jnp.concatenate(parts, axis=1)
    assert packed.shape == (d, C_END)
    return packed.astype(BF16)


def kernel(x_prompt, x_sample, cache_k, cache_v, cache_kidx, state_pool, state_conv, page_table, norm1_g,
           w_in, b_gate, pool_mix_w, pool_scale, w_pool_o, w_attn_o, w_out, norm2_g, w_up, conv_w, conv_b,
           w_down, normf_g):
    depth = w_in.shape[0]
    b, l, d = x_prompt.shape
    db, dl, _ = x_sample.shape
    n_pages = page_table.shape[1]
    page = cache_k.shape[2]
    past = n_pages * page
    c2 = w_up.shape[2]
    assert depth == 1 and d == 1024 and dl % SUBLANES == 0 and l % KEY_CHUNK == 0

    cos_p, sin_p, pos1_p = _rope_tables(jnp.arange(l))
    cos_s, sin_s, pos1_s = _rope_tables(past + jnp.arange(dl))
    topk_p = min(INDEX_TOPK, l // 4)
    topk_s = min(INDEX_TOPK, (past + dl) // 4)
    sample_bt = min(db, 256 // dl)

    cache_kt = jnp.transpose(cache_k, (0, 1, 3, 4, 2))
    cache_vt = jnp.transpose(cache_v, (0, 1, 3, 4, 2))
    cache_kit = jnp.transpose(cache_kidx, (0, 1, 3, 2))

    li = 0
    w_all = _pack_w_in(w_in[li])
    g1 = norm1_g[li][None]
    bg = b_gate[li][None]
    mixw = pool_mix_w[li].astype(BF16)
    pscale = pool_scale[li][None]
    wpo = w_pool_o[li].astype(BF16)
    wao = w_attn_o[li].astype(BF16)
    wout = w_out[li].astype(BF16)
    g2 = norm2_g[li][None]
    wup = w_up[li].astype(BF16)
    wdn = w_down[li].astype(BF16)
    cw = conv_w[li]
    cb = conv_b[li][None]
    gf = normf_g[None]

    (pa, gb, q, qi, misc, ptail, kt, kb, vt, vtb, kit, ki4) = _mixer_in(
        x_prompt, jnp.zeros((b, POOL_PAD, D_POOL), F32), cos_p, sin_p, pos1_p, g1, w_all, bg, mixw, pscale,
        wpo, bt=1, lt=KEY_CHUNK, prompt=True)
    attn = _prompt_attention(qi, misc, q, ki4, kb, vtb, topk=topk_p)
    y_p, ctail = _output_stage(x_prompt, pa, gb, attn, jnp.zeros((b, CONV_PAD, c2), F32), wao, wout, g2,
                               wup, cw, cb, wdn, gf, bt=1, lt=256, ff_chunk=256)
    outs_p = (jnp.transpose(kt.reshape(b, N_HEADS, HEAD_DIM, l), (0, 3, 1, 2))[None],
              jnp.transpose(vt.reshape(b, N_HEADS, HEAD_DIM, l), (0, 3, 1, 2))[None],
              jnp.transpose(kit, (0, 2, 1))[None],
              ptail[:, 1:][None], ctail[:, CONV_PAD - 2:][None])

    pre_pool = jnp.concatenate([jnp.zeros((db, 1, D_POOL), F32), state_pool[li]], axis=1)
    (pa, gb, q, qi, misc, ptail, k, v, ki) = _mixer_in(
        x_sample, pre_pool, cos_s, sin_s, pos1_s, g1, w_all, bg, mixw, pscale, wpo,
        bt=sample_bt, lt=dl, prompt=False)
    keep = _sample_index(page_table, qi, misc, ki, cache_kit, li, topk=topk_s, pages_per_step=16)
    attn = _sample_attention(page_table, q, keep, k, v, cache_kt, cache_vt, li, pages_per_step=8)
    pre_conv = jnp.concatenate([jnp.zeros((db, CONV_PAD - 2, c2), F32), state_conv[li]], axis=1)
    y_s, ctail = _output_stage(x_sample, pa, gb, attn, pre_conv, wao, wout, g2, wup, cw, cb, wdn, gf,
                               bt=sample_bt, lt=dl, ff_chunk=256)
    outs_s = (k.reshape(db, dl, N_HEADS, HEAD_DIM)[None], v.reshape(db, dl, N_HEADS, HEAD_DIM)[None],
              ki[None], ptail[:, 1:][None], ctail[:, CONV_PAD - 2:][None])

    return (y_p, y_s, *outs_p, *outs_s)
```

```python
import functools
import math

import jax
import jax.numpy as jnp
from jax import lax
from jax.experimental import pallas as pl
from jax.experimental.pallas import tpu as pltpu

F32 = jnp.float32
BF16 = jnp.bfloat16
I32 = jnp.int32

LANES = 128
SUBLANES = 8
VMEM_LIMIT_BYTES = 56 * 1024 * 1024

N_HEADS = 8
HEAD_DIM = 64
D_ATTN = N_HEADS * HEAD_DIM
IDX_HEADS = 8
IDX_DIM = 64
D_POOL = 512
POOL_WINDOWS = (2, 4, 8, 16)
POOL_PAD = 16
CONV_PAD = 8
INDEX_TOPK = 256
Q_BLOCK = 128
KEY_CHUNK = 256
ROPE_THETA = 10000.0
RMS_EPS = 1e-6
MASK_BIAS = -1e30
INT_MIN = -(2 ** 31)
NEG_INF = float("-inf")

C_P, C_Q, C_K, C_V, C_QI = 0, 512, 1024, 1536, 2048
C_KI4 = 2560
C_IW = 2816
C_GA = 2944
C_GB = 3968
C_END = 4992


def _params():
    return pltpu.CompilerParams(dimension_semantics=("arbitrary", "arbitrary"),
                                vmem_limit_bytes=VMEM_LIMIT_BYTES)


def _rms(x, g):
    return x * lax.rsqrt(jnp.mean(x * x, axis=-1, keepdims=True) + RMS_EPS) * g


def _rope_blocks(z, cos, sin, first_half):
    outs = []
    for c in range(z.shape[1] // LANES):
        zb = z[:, c * LANES:(c + 1) * LANES]
        sw = jnp.where(first_half, pltpu.roll(zb, LANES - 32, 1), pltpu.roll(zb, 32, 1))
        outs.append(zb * cos + sw * sin)
    return outs[0] if len(outs) == 1 else jnp.concatenate(outs, axis=1)


def _bdot(a, b):
    return jnp.dot(a, b, preferred_element_type=F32)


def _dot_nt(a, b):
    return lax.dot_general(a, b, (((1,), (1,)), ((), ())), preferred_element_type=F32)


def _split(x):
    hi = x.astype(BF16)
    return hi, (x - hi.astype(F32)).astype(BF16)


def _mixer_in_kernel(x_ref, pre_ref, cos_ref, sin_ref, pos1_ref, g1_ref, w_ref, bg_ref, mixw_ref,
                     pscale_ref, wpo_ref, pa_ref, gb_ref, q_ref, qi_ref, misc_ref, tail_ref, *rest, prompt):
    carry_ref = rest[-1]
    j = pl.program_id(1)
    bt, lt, d = x_ref.shape
    m = bt * lt
    narrow = pa_ref.dtype

    x = x_ref[...].reshape(m, d)
    h = _rms(x, g1_ref[...]).astype(BF16)

    def rows(ref):
        t = ref[...]
        if bt == 1:
            return t
        return jnp.broadcast_to(t[None], (bt, lt, LANES)).reshape(m, LANES)

    cos, sin, pos1 = rows(cos_ref), rows(sin_ref), rows(pos1_ref)
    lane = lax.broadcasted_iota(I32, (m, LANES), 1)
    first_half = (lane % 64) < 32

    def proj(c0, c1):
        return _bdot(h, w_ref[:, c0:c1])

    p = proj(C_P, C_Q)

    @pl.when(j == 0)
    def _():
        carry_ref[...] = pre_ref[...]

    ext = jnp.concatenate([carry_ref[...], p.reshape(bt, lt, D_POOL)], axis=1)
    tail = ext[:, lt:, :]
    carry_ref[...] = tail
    tail_ref[...] = tail
    e2 = ext.reshape(bt * (POOL_PAD + lt), D_POOL)
    pools = []
    for g, w in enumerate(POOL_WINDOWS):
        s = e2[:, g * LANES:(g + 1) * LANES]
        sh = 1
        while sh < w:
            s = s + pltpu.roll(s, sh, 0)
            sh *= 2
        wsum = s.reshape(bt, POOL_PAD + lt, LANES)[:, POOL_PAD:, :].reshape(m, LANES)
        dlt = wsum / jnp.minimum(pos1, float(w)) - p[:, g * LANES:(g + 1) * LANES]
        pools.append(_bdot(dlt.astype(BF16), mixw_ref[g]))
    pool = jnp.concatenate(pools, axis=1) * pscale_ref[...]
    a = _bdot(pool.astype(BF16), wpo_ref[...])

    bg = bg_ref[...]
    ga = jax.nn.sigmoid(proj(C_GA, C_GB) + bg[:, :d])
    pa_ref[...] = (ga * a).astype(narrow).reshape(bt, lt, d)
    gb = jax.nn.sigmoid(proj(C_GB, C_END) + bg[:, d:])
    gb_ref[...] = gb.astype(narrow).reshape(bt, lt, d)

    q = _rope_blocks(proj(C_Q, C_K), cos, sin, first_half)
    q_ref[...] = q.astype(narrow).reshape(bt, lt, D_ATTN)
    qi = _rope_blocks(proj(C_QI, C_KI4), cos, sin, first_half)
    qi_ref[...] = qi.reshape(bt, lt, 512)
    misc_ref[...] = (proj(C_IW, C_GA) * (IDX_HEADS ** -0.5 * IDX_DIM ** -0.5)).reshape(bt, lt, LANES)
    k = _rope_blocks(proj(C_K, C_V), cos, sin, first_half)
    v = proj(C_V, C_QI)
    ki4 = _rope_blocks(proj(C_KI4, C_IW), cos, sin, first_half)
    if prompt:
        kt_ref, kb_ref, vt_ref, vtb_ref, kit_ref, ki4_ref = rest[:-1]
        kt_ref[0] = k.T
        kb_ref[...] = k.astype(BF16).reshape(bt, lt, D_ATTN)
        vt = v.T
        vt_ref[0] = vt
        vtb_ref[0, 0] = vt.astype(BF16)
        kit_ref[0] = ki4[:, :LANES].T[:IDX_DIM, :]
        hi, lo = _split(ki4)
        ki4_ref[...] = jnp.concatenate([hi[:, :LANES], lo[:, LANES:]], axis=1).reshape(bt, lt, 256)
    else:
        k_ref, v_ref, ki_ref = rest[:-1]
        k_ref[...] = k.reshape(bt, lt, D_ATTN)
        v_ref[...] = v.reshape(bt, lt, D_ATTN)
        ki_ref[...] = ki4[:, :IDX_DIM].reshape(bt, lt, IDX_DIM)


def _mixer_in(x, prefix16, cos, sin, pos1, g1, w_all, bg, mixw, pscale, wpo, *, bt, lt, prompt):
    b, l, d = x.shape
    nb, nj = b // bt, l // lt
    narrow = BF16 if prompt else F32
    tok = lambda c: pl.BlockSpec((bt, lt, c), lambda i, j: (i, j, 0))
    full = lambda a: pl.BlockSpec(a.shape, lambda i, j: (0,) * a.ndim)
    tab = pl.BlockSpec((lt, LANES), lambda i, j: (j, 0))
    pre = pl.BlockSpec((bt, POOL_PAD, D_POOL), lambda i, j: (i, 0, 0))
    sds = lambda c, dt: jax.ShapeDtypeStruct((b, l, c), dt)
    out_shape = [sds(d, narrow), sds(d, narrow), sds(D_ATTN, narrow), sds(512, F32), sds(LANES, F32),
                 jax.ShapeDtypeStruct((b, POOL_PAD, D_POOL), F32)]
    out_specs = [tok(d), tok(d), tok(D_ATTN), tok(512), tok(LANES), pre]
    if prompt:
        assert bt == 1
        tr = lambda c: pl.BlockSpec((1, c, lt), lambda i, j: (i, 0, j))
        trs = lambda c: jax.ShapeDtypeStruct((b, c, l), F32)
        out_shape += [trs(D_ATTN), sds(D_ATTN, BF16), trs(D_ATTN),
                      jax.ShapeDtypeStruct((b, nj, D_ATTN, lt), BF16), trs(IDX_DIM), sds(256, BF16)]
        out_specs += [tr(D_ATTN), tok(D_ATTN), tr(D_ATTN),
                      pl.BlockSpec((1, 1, D_ATTN, lt), lambda i, j: (i, j, 0, 0)), tr(IDX_DIM), tok(256)]
    else:
        out_shape += [sds(D_ATTN, F32), sds(D_ATTN, F32), sds(IDX_DIM, F32)]
        out_specs += [tok(D_ATTN), tok(D_ATTN), tok(IDX_DIM)]
    return pl.pallas_call(
        functools.partial(_mixer_in_kernel, prompt=prompt),
        grid=(nb, nj),
        in_specs=[tok(d), pre, tab, tab, tab,
                  full(g1), full(w_all), full(bg), full(mixw), full(pscale), full(wpo)],
        out_specs=tuple(out_specs),
        out_shape=tuple(out_shape),
        scratch_shapes=[pltpu.VMEM((bt, POOL_PAD, D_POOL), F32)],
        compiler_params=_params(),
        name="mixer_in",
    )(x, prefix16, cos, sin, pos1, g1, w_all, bg, mixw, pscale, wpo)


def _ordinal_to_float(u):
    key = u ^ INT_MIN
    return lax.bitcast_convert_type(key ^ ((key >> 31) & 0x7FFFFFFF), F32)


def _count(key_ref, n_chunks, chunk, pred):
    width = key_ref.shape[1]

    def body(c, acc):
        r0 = pl.multiple_of(c * chunk, chunk)
        kc = key_ref[pl.ds(r0, chunk), :]
        hit = jnp.where(pred(kc, r0), 1, 0).astype(I32)
        return acc + jnp.sum(hit.reshape(chunk // SUBLANES, SUBLANES, width), axis=0)

    acc = lax.fori_loop(0, n_chunks, body, jnp.zeros((SUBLANES, width), I32))
    return jnp.sum(acc, axis=0, keepdims=True)


def _topk_select(key_ref, n_chunks, chunk, topk, idx_bits):
    width = key_ref.shape[1]

    def bit_body(t, ans):
        cand = ans | jnp.left_shift(jnp.int32(1), 31 - t)
        ck = _ordinal_to_float(cand)
        cnt = _count(key_ref, n_chunks, chunk, lambda kc, r0: kc >= ck)
        return jnp.where(cnt >= topk, cand, ans)

    thr = _ordinal_to_float(lax.fori_loop(0, 32, bit_body, jnp.zeros((1, width), I32)))
    n_gt = _count(key_ref, n_chunks, chunk, lambda kc, r0: kc > thr)
    n_eq = _count(key_ref, n_chunks, chunk, lambda kc, r0: kc == thr)
    need = topk - n_gt
    big = jnp.full((1, width), 2 ** idx_bits, I32)

    def tie_search():
        def idx_body(t, xs):
            cand = xs | jnp.left_shift(jnp.int32(1), idx_bits - 1 - t)

            def pred(kc, r0):
                row = r0 + lax.broadcasted_iota(I32, kc.shape, 0)
                return (kc == thr) & (row < cand)

            cnt = _count(key_ref, n_chunks, chunk, pred)
            return jnp.where(cnt < need, cand, xs)

        return lax.fori_loop(0, idx_bits, idx_body, jnp.zeros((1, width), I32))

    has_tie = jnp.max(n_eq - need) > 0
    cut = lax.cond(has_tie, tie_search, lambda: big)
    return thr, cut


def _prompt_attn_kernel(qi_ref, misc_ref, q_ref, ki4_ref, kb_ref, vt_ref, o_ref,
                        key_ref, bias_ref, rhs_ref, qbd_ref, acc_ref, *, topk):
    i = pl.program_id(1)
    qb = q_ref.shape[1]
    lk = ki4_ref.shape[1]
    chunk = vt_ref.shape[3]
    n_pairs = N_HEADS // 2
    n_chunks = (i * qb + qb + chunk - 1) // chunk
    q_pos = i * qb + lax.broadcasted_iota(I32, (1, qb), 1)

    qhi, qlo = _split(qi_ref[0].T)
    iwt = misc_ref[0].T[:IDX_HEADS, :]

    def head_rhs(hh):
        a, b = qhi[hh * 64:(hh + 1) * 64], qlo[hh * 64:(hh + 1) * 64]
        return jnp.concatenate([a, b, a, b], axis=0)

    for jp in range(n_pairs):
        rhs_ref[jp] = jnp.concatenate([head_rhs(2 * jp), head_rhs(2 * jp + 1)], axis=1)

    def score_body(c, carry):
        r0 = pl.multiple_of(c * chunk, chunk)
        kc = ki4_ref[0, pl.ds(r0, chunk), :]
        sc = jnp.zeros((chunk, qb), F32)
        for jp in range(n_pairs):
            dots = jnp.maximum(_bdot(kc, rhs_ref[jp]), 0.0)
            sc = sc + dots[:, :qb] * iwt[2 * jp:2 * jp + 1, :] + dots[:, qb:] * iwt[2 * jp + 1:2 * jp + 2, :]
        row = r0 + lax.broadcasted_iota(I32, (chunk, qb), 0)
        key_ref[pl.ds(r0, chunk), :] = jnp.where(row <= q_pos, sc, NEG_INF)
        return carry

    lax.fori_loop(0, n_chunks, score_body, 0)

    keep_all = (jnp.full((1, qb), NEG_INF, F32), jnp.full((1, qb), lk, I32))

    def searched():
        thr, cut = _topk_select(key_ref, n_chunks, chunk, topk, int(math.log2(lk)))
        few = q_pos < topk
        return jnp.where(few, keep_all[0], thr), jnp.where(few, keep_all[1], cut)

    thr, cut = lax.cond((i * qb + qb) > topk, searched, lambda: keep_all)

    def bias_body(c, carry):
        r0 = pl.multiple_of(c * chunk, chunk)
        kc = key_ref[pl.ds(r0, chunk), :]
        row = r0 + lax.broadcasted_iota(I32, (chunk, qb), 0)
        keep = ((kc > thr) | ((kc == thr) & (row <= cut))) & (row <= q_pos)
        bias_ref[pl.ds(r0, chunk), :] = jnp.where(keep, 0.0, MASK_BIAS)
        return carry

    lax.fori_loop(0, n_chunks, bias_body, 0)

    qt = q_ref[0].astype(F32).T.astype(BF16)
    zero = jnp.zeros((HEAD_DIM, qb), BF16)
    for jp in range(n_pairs):
        h0, h1 = 2 * jp, 2 * jp + 1
        qbd_ref[jp] = jnp.concatenate(
            [jnp.concatenate([qt[h0 * 64:(h0 + 1) * 64], zero], axis=1),
             jnp.concatenate([zero, qt[h1 * 64:(h1 + 1) * 64]], axis=1)], axis=0)
    acc_ref[...] = jnp.zeros(acc_ref.shape, F32)

    def attn_body(c, carry):
        r0 = pl.multiple_of(c * chunk, chunk)
        bias = bias_ref[pl.ds(r0, chunk), :]
        bias2 = jnp.concatenate([bias, bias], axis=1)
        s_all = [_bdot(kb_ref[0, pl.ds(r0, chunk), jp * LANES:(jp + 1) * LANES], qbd_ref[jp])
                 for jp in range(n_pairs)]
        new, ps, corrs = [], [], []
        for jp in range(n_pairs):
            m_run, l_run = carry[jp]
            s = s_all[jp] + bias2
            m_new = jnp.maximum(m_run, jnp.max(s, axis=0, keepdims=True))
            p = jnp.exp(s - m_new)
            corr = jnp.exp(m_run - m_new)
            new.append((m_new, l_run * corr + jnp.sum(p, axis=0, keepdims=True)))
            ps.append(p.astype(BF16))
            corrs.append(corr)
        pv = [_bdot(vt_ref[0, c, jp * LANES:(jp + 1) * LANES, :], ps[jp]) for jp in range(n_pairs)]
        for jp in range(n_pairs):
            acc_ref[jp] = acc_ref[jp] * corrs[jp] + pv[jp]
        return tuple(new)

    init = tuple((jnp.full((1, 2 * qb), MASK_BIAS, F32), jnp.zeros((1, 2 * qb), F32))
                 for _ in range(n_pairs))
    fin = lax.fori_loop(0, n_chunks, attn_body, init)
    outs = []
    for jp in range(n_pairs):
        o = acc_ref[jp] / fin[jp][1]
        outs.append(o[:HEAD_DIM, :qb])
        outs.append(o[HEAD_DIM:, qb:])
    o_ref[0] = jnp.concatenate(outs, axis=0).T.astype(o_ref.dtype)


def _prompt_attention(qi, misc, q, ki4, kb, vt, *, topk):
    b, l, _ = q.shape
    nq = l // Q_BLOCK
    blk = lambda c: pl.BlockSpec((1, Q_BLOCK, c), lambda i, j: (i, j, 0))
    seq = lambda c: pl.BlockSpec((1, l, c), lambda i, j: (i, 0, 0))
    n_pairs = N_HEADS // 2
    return pl.pallas_call(
        functools.partial(_prompt_attn_kernel, topk=topk),
        grid=(b, nq),
        in_specs=[blk(512), blk(LANES), blk(D_ATTN), seq(256), seq(D_ATTN),
                  pl.BlockSpec((1,) + vt.shape[1:], lambda i, j: (i, 0, 0, 0))],
        out_specs=blk(D_ATTN),
        out_shape=jax.ShapeDtypeStruct((b, l, D_ATTN), BF16),
        scratch_shapes=[pltpu.VMEM((l, Q_BLOCK), F32), pltpu.VMEM((l, Q_BLOCK), F32),
                        pltpu.VMEM((n_pairs, 4 * IDX_DIM, 2 * Q_BLOCK), BF16),
                        pltpu.VMEM((n_pairs, 2 * HEAD_DIM, 2 * Q_BLOCK), BF16),
                        pltpu.VMEM((n_pairs, 2 * HEAD_DIM, 2 * Q_BLOCK), F32)],
        compiler_params=_params(),
        name="prompt_attention",
    )(qi, misc, q, ki4, kb, vt)


def _gelu_tanh(x):
    return 0.5 * x * (1.0 + jnp.tanh(math.sqrt(2.0 / math.pi) * (x + 0.044715 * (x * x * x))))


def _output_kernel(x_ref, pa_ref, gb_ref, at_ref, pre_ref, wao_ref, wout_ref, g2_ref, wup_ref, cw_ref,
                   cb_ref, wdn_ref, gf_ref, y_ref, tail_ref, carry_ref, act_ref, *, ff_chunk):
    j = pl.program_id(1)
    bt, lt, d = x_ref.shape
    m = bt * lt
    dff = wdn_ref.shape[0]

    x = x_ref[...].reshape(m, d)
    at = at_ref[...].reshape(m, D_ATTN).astype(BF16)
    mrg = (pa_ref[...].reshape(m, d).astype(F32)
           + gb_ref[...].reshape(m, d).astype(F32) * _bdot(at, wao_ref[...]))
    x1 = x + _bdot(mrg.astype(BF16), wout_ref[...])
    h2 = _rms(x1, g2_ref[...]).astype(BF16)

    @pl.when(j == 0)
    def _():
        carry_ref[...] = pre_ref[...]

    def conv(c0):
        u = _bdot(h2, wup_ref[:, c0:c0 + ff_chunk])
        ext = jnp.concatenate([carry_ref[:, :, c0:c0 + ff_chunk], u.reshape(bt, lt, ff_chunk)], axis=1)
        carry_ref[:, :, c0:c0 + ff_chunk] = ext[:, lt:, :]
        e2 = ext.reshape(bt * (CONV_PAD + lt), ff_chunk)

        def shifted(e):
            return e.reshape(bt, CONV_PAD + lt, ff_chunk)[:, CONV_PAD:, :].reshape(m, ff_chunk)

        cw = cw_ref[:, c0:c0 + ff_chunk]
        return (cb_ref[:, c0:c0 + ff_chunk] + cw[0:1] * shifted(pltpu.roll(e2, 2, 0))
                + cw[1:2] * shifted(pltpu.roll(e2, 1, 0)) + cw[2:3] * u)

    for cc in range(dff // ff_chunk):
        gate = conv(cc * ff_chunk)
        val = conv(dff + cc * ff_chunk)
        act_ref[:, cc * ff_chunk:(cc + 1) * ff_chunk] = (_gelu_tanh(gate) * val).astype(BF16)
    tail_ref[...] = carry_ref[...]
    y_ref[...] = _rms(x1 + _bdot(act_ref[...], wdn_ref[...]), gf_ref[...]).reshape(bt, lt, d)


def _output_stage(x, pa, gb, attn, prefix8, wao, wout, g2, wup, cw, cb, wdn, gf, *, bt, lt, ff_chunk):
    b, l, d = x.shape
    nb, nj = b // bt, l // lt
    c2 = wup.shape[1]
    tok = lambda c: pl.BlockSpec((bt, lt, c), lambda i, j: (i, j, 0))
    full = lambda a: pl.BlockSpec(a.shape, lambda i, j: (0,) * a.ndim)
    pre = pl.BlockSpec((bt, CONV_PAD, c2), lambda i, j: (i, 0, 0))
    return pl.pallas_call(
        functools.partial(_output_kernel, ff_chunk=ff_chunk),
        grid=(nb, nj),
        in_specs=[tok(d), tok(d), tok(d), tok(D_ATTN), pre, full(wao), full(wout), full(g2), full(wup),
                  full(cw), full(cb), full(wdn), full(gf)],
        out_specs=(tok(d), pre),
        out_shape=(jax.ShapeDtypeStruct((b, l, d), F32), jax.ShapeDtypeStruct((b, CONV_PAD, c2), F32)),
        scratch_shapes=[pltpu.VMEM((bt, CONV_PAD, c2), F32), pltpu.VMEM((bt * lt, c2 // 2), BF16)],
        compiler_params=_params(),
        name="output_stage",
    )(x, pa, gb, attn, prefix8, wao, wout, g2, wup, cw, cb, wdn, gf)


def _sample_index_kernel(pt_ref, qi_ref, misc_ref, kin_ref, *rest, topk, pages_per_step):
    page_refs = rest[:pages_per_step]
    keep_ref, key_ref = rest[pages_per_step], rest[pages_per_step + 1]
    g = pl.program_id(1)
    ng = pl.num_programs(1)
    nq = qi_ref.shape[1]
    page = page_refs[0].shape[3]
    n_pages = key_ref.shape[0] - 1
    past_len = n_pages * page

    qi = qi_ref[0]
    qrows = jnp.concatenate([qi[:, hh * IDX_DIM:(hh + 1) * IDX_DIM] for hh in range(IDX_HEADS)], axis=0)
    qhi, qlo = _split(qrows)
    iw = misc_ref[0][:, :IDX_HEADS]

    def scores(kt):
        khi, klo = _split(kt)
        dots = _bdot(qhi, khi) + _bdot(qhi, klo) + _bdot(qlo, khi) + _bdot(qlo, klo)
        dots = jnp.maximum(dots, 0.0)
        sc = jnp.zeros((nq, kt.shape[1]), F32)
        for hh in range(IDX_HEADS):
            sc = sc + dots[hh * nq:(hh + 1) * nq, :] * iw[:, hh:hh + 1]
        return sc

    step_scores = scores(jnp.concatenate([r[0, 0] for r in page_refs], axis=1))
    for pp in range(pages_per_step):
        key_ref[g * pages_per_step + pp] = step_scores[:, pp * page:(pp + 1) * page]

    @pl.when(g == ng - 1)
    def _():
        kin = jnp.concatenate([kin_ref[0], jnp.zeros((page - nq, IDX_DIM), F32)], axis=0)
        qrow2 = lax.broadcasted_iota(I32, (nq, page), 0)
        col2 = lax.broadcasted_iota(I32, (nq, page), 1)
        key_ref[n_pages] = jnp.where(col2 <= qrow2, scores(kin.T), NEG_INF)

        keys = key_ref[...]
        shape = keys.shape
        col = lax.broadcasted_iota(I32, shape, 0) * page + lax.broadcasted_iota(I32, shape, 2)
        qrow = lax.broadcasted_iota(I32, shape, 1)

        def count(pred):
            return jnp.sum(jnp.sum(jnp.where(pred, 1, 0).astype(I32), axis=0), axis=1, keepdims=True)

        def bit_body(t, ans):
            cand = ans | jnp.left_shift(jnp.int32(1), 31 - t)
            return jnp.where(count(keys >= _ordinal_to_float(cand)) >= topk, cand, ans)

        thr = _ordinal_to_float(lax.fori_loop(0, 32, bit_body, jnp.zeros((nq, 1), I32)))
        need = topk - count(keys > thr)
        eq = keys == thr
        idx_bits = int(math.ceil(math.log2(past_len + page)))

        def tie_search():
            def idx_body(t, xs):
                cand = xs | jnp.left_shift(jnp.int32(1), idx_bits - 1 - t)
                return jnp.where(count(eq & (col < cand)) < need, cand, xs)

            return lax.fori_loop(0, idx_bits, idx_body, jnp.zeros((nq, 1), I32))

        has_tie = jnp.max(count(eq) - need) > 0
        cut = lax.cond(has_tie, tie_search, lambda: jnp.full((nq, 1), 2 ** idx_bits, I32))
        keep = ((keys > thr) | (eq & (col <= cut))) & (col <= past_len + qrow)
        keep_ref[0] = jnp.where(keep, 1.0, 0.0)


def _sample_index(page_table, qi, misc, ki_new, cache_kidx_t, layer, *, topk, pages_per_step):
    b, nq, _ = qi.shape
    n_pages = page_table.shape[1]
    page = cache_kidx_t.shape[3]
    assert n_pages * page >= topk
    pages_per_step = min(pages_per_step, n_pages)
    ng = n_pages // pages_per_step

    def page_spec(pp):
        return pl.BlockSpec((1, 1, IDX_DIM, page),
                            lambda i, g, pt: (layer, pt[i * n_pages + g * pages_per_step + pp], 0, 0))

    seq = lambda c: pl.BlockSpec((1, nq, c), lambda i, g, pt: (i, 0, 0))
    grid_spec = pltpu.PrefetchScalarGridSpec(
        num_scalar_prefetch=1,
        grid=(b, ng),
        in_specs=[seq(512), seq(LANES), seq(IDX_DIM)] + [page_spec(pp) for pp in range(pages_per_step)],
        out_specs=pl.BlockSpec((1, n_pages + 1, nq, page), lambda i, g, pt: (i, 0, 0, 0)),
        scratch_shapes=[pltpu.VMEM((n_pages + 1, nq, page), F32)],
    )
    return pl.pallas_call(
        functools.partial(_sample_index_kernel, topk=topk, pages_per_step=pages_per_step),
        grid_spec=grid_spec,
        out_shape=jax.ShapeDtypeStruct((b, n_pages + 1, nq, page), F32),
        compiler_params=_params(),
        name="sample_index",
    )(page_table.reshape(-1), qi, misc, ki_new, *([cache_kidx_t] * pages_per_step))


def _sample_attn_kernel(pt_ref, q_ref, keep_ref, keepn_ref, kn_ref, vn_ref, *rest, pages_per_step):
    k_refs = rest[:pages_per_step]
    v_refs = rest[pages_per_step:2 * pages_per_step]
    o_ref, m_ref, l_ref, acc_ref = rest[2 * pages_per_step:]
    g = pl.program_id(1)
    ng = pl.num_programs(1)
    nq = q_ref.shape[1]
    page = k_refs[0].shape[4]

    q = q_ref[0].astype(BF16)
    qh = [q[:, hh * HEAD_DIM:(hh + 1) * HEAD_DIM] for hh in range(N_HEADS)]

    @pl.when(g == 0)
    def _():
        m_ref[...] = jnp.full(m_ref.shape, MASK_BIAS, F32)
        l_ref[...] = jnp.zeros(l_ref.shape, F32)
        acc_ref[...] = jnp.zeros(acc_ref.shape, F32)

    def update(score_h, out_h, keep):
        s = jnp.concatenate([score_h(hh) for hh in range(N_HEADS)], axis=0)
        s = jnp.where(jnp.concatenate([keep] * N_HEADS, axis=0) > 0.5, s, MASK_BIAS)
        m_old = m_ref[...]
        m_new = jnp.maximum(m_old, jnp.max(s, axis=1, keepdims=True))
        p = jnp.exp(s - m_new)
        corr = jnp.exp(m_old - m_new)
        l_ref[...] = l_ref[...] * corr + jnp.sum(p, axis=1, keepdims=True)
        pb = p.astype(BF16)
        pv = jnp.concatenate([out_h(hh, pb[hh * nq:(hh + 1) * nq]) for hh in range(N_HEADS)], axis=0)
        acc_ref[...] = acc_ref[...] * corr + pv
        m_ref[...] = m_new

    def head_cat(refs, hh):
        return jnp.concatenate([r[0, 0, hh] for r in refs], axis=1).astype(BF16)

    update(lambda hh: _bdot(qh[hh], head_cat(k_refs, hh)),
           lambda hh, p: _dot_nt(p, head_cat(v_refs, hh)),
           jnp.concatenate([keep_ref[0, pp] for pp in range(pages_per_step)], axis=1))

    @pl.when(g == ng - 1)
    def _():
        pad = jnp.zeros((page - nq, D_ATTN), F32)
        kn = jnp.concatenate([kn_ref[0], pad], axis=0).astype(BF16)
        vn = jnp.concatenate([vn_ref[0], pad], axis=0).astype(BF16)
        update(lambda hh: _dot_nt(qh[hh], kn[:, hh * HEAD_DIM:(hh + 1) * HEAD_DIM]),
               lambda hh, p: _bdot(p, vn[:, hh * HEAD_DIM:(hh + 1) * HEAD_DIM]), keepn_ref[0, 0])
        o = acc_ref[...] / l_ref[...]
        o_ref[0] = jnp.concatenate([o[hh * nq:(hh + 1) * nq, :] for hh in range(N_HEADS)], axis=1)


def _sample_attention(page_table, q, keep, k_new, v_new, cache_kt, cache_vt, layer, *, pages_per_step):
    b, nq, _ = q.shape
    n_pages = page_table.shape[1]
    page = cache_kt.shape[4]
    pages_per_step = min(pages_per_step, n_pages)
    ng = n_pages // pages_per_step

    def page_spec(pp):
        return pl.BlockSpec((1, 1, N_HEADS, HEAD_DIM, page),
                            lambda i, g, pt: (layer, pt[i * n_pages + g * pages_per_step + pp], 0, 0, 0))

    seq = lambda c: pl.BlockSpec((1, nq, c), lambda i, g, pt: (i, 0, 0))
    grid_spec = pltpu.PrefetchScalarGridSpec(
        num_scalar_prefetch=1,
        grid=(b, ng),
        in_specs=[seq(D_ATTN),
                  pl.BlockSpec((1, pages_per_step, nq, page), lambda i, g, pt: (i, g, 0, 0)),
                  pl.BlockSpec((1, 1, nq, page), lambda i, g, pt: (i, n_pages, 0, 0)),
                  seq(D_ATTN), seq(D_ATTN)] + [page_spec(pp) for pp in range(pages_per_step)] * 2,
        out_specs=seq(D_ATTN),
        scratch_shapes=[pltpu.VMEM((N_HEADS * nq, 1), F32), pltpu.VMEM((N_HEADS * nq, 1), F32),
                        pltpu.VMEM((N_HEADS * nq, HEAD_DIM), F32)],
    )
    return pl.pallas_call(
        functools.partial(_sample_attn_kernel, pages_per_step=pages_per_step),
        grid_spec=grid_spec,
        out_shape=jax.ShapeDtypeStruct((b, nq, D_ATTN), F32),
        compiler_params=_params(),
        name="sample_attention",
    )(page_table.reshape(-1), q, keep, keep, k_new, v_new,
      *([cache_kt] * pages_per_step), *([cache_vt] * pages_per_step))


def _rope_tables(pos):
    half = HEAD_DIM // 2
    freq = ROPE_THETA ** (-jnp.arange(half, dtype=F32) / half)
    ang = pos.astype(F32)[:, None] * freq[None, :]
    cos, sin = jnp.cos(ang), jnp.sin(ang)
    cos = jnp.tile(jnp.concatenate([cos, cos], axis=-1), (1, LANES // HEAD_DIM))
    sin = jnp.tile(jnp.concatenate([-sin, sin], axis=-1), (1, LANES // HEAD_DIM))
    pos1 = jnp.broadcast_to((pos + 1).astype(F32)[:, None], (pos.shape[0], LANES))
    return cos, sin, pos1


def _pack_w_in(w):
    d = w.shape[0]
    o_q, o_k, o_v, o_qi = D_POOL, D_POOL + D_ATTN, D_POOL + 2 * D_ATTN, D_POOL + 3 * D_ATTN
    o_ki = o_qi + IDX_HEADS * IDX_DIM
    o_iw = o_ki + IDX_DIM
    o_g = o_iw + IDX_HEADS
    parts = [w[:, :o_q], w[:, o_q:o_k] * (HEAD_DIM ** -0.5), w[:, o_k:o_qi],
             w[:, o_qi:o_ki], jnp.tile(w[:, o_ki:o_iw], (1, 4)),
             jnp.pad(w[:, o_iw:o_g], ((0, 0), (0, LANES - IDX_HEADS))), w[:, o_g:]]
    packed = jnp.concatenate(parts, axis=1)
    assert packed.shape == (d, C_END)
    return packed.astype(BF16)


def kernel(x_prompt, x_sample, cache_k, cache_v, cache_kidx, state_pool, state_conv, page_table, norm1_g,
           w_in, b_gate, pool_mix_w, pool_scale, w_pool_o, w_attn_o, w_out, norm2_g, w_up, conv_w, conv_b,
           w_down, normf_g):
    depth = w_in.shape[0]
    b, l, d = x_prompt.shape
    db, dl, _ = x_sample.shape
    n_pages = page_table.shape[1]
    page = cache_k.shape[2]
    past = n_pages * page
    c2 = w_up.shape[2]
    assert depth == 1 and d == 1024 and dl % SUBLANES == 0 and l % KEY_CHUNK == 0

    cos_p, sin_p, pos1_p = _rope_tables(jnp.arange(l))
    cos_s, sin_s, pos1_s = _rope_tables(past + jnp.arange(dl))
    topk_p = min(INDEX_TOPK, l // 4)
    topk_s = min(INDEX_TOPK, (past + dl) // 4)
    sample_bt = min(db, 256 // dl)

    cache_kt = jnp.transpose(cache_k, (0, 1, 3, 4, 2))
    cache_vt = jnp.transpose(cache_v, (0, 1, 3, 4, 2))
    cache_kit = jnp.transpose(cache_kidx, (0, 1, 3, 2))

    li = 0
    w_all = _pack_w_in(w_in[li])
    g1 = norm1_g[li][None]
    bg = b_gate[li][None]
    mixw = pool_mix_w[li].astype(BF16)
    pscale = pool_scale[li][None]
    wpo = w_pool_o[li].astype(BF16)
    wao = w_attn_o[li].astype(BF16)
    wout = w_out[li].astype(BF16)
    g2 = norm2_g[li][None]
    wup = w_up[li].astype(BF16)
    wdn = w_down[li].astype(BF16)
    cw = conv_w[li]
    cb = conv_b[li][None]
    gf = normf_g[None]

    (pa, gb, q, qi, misc, ptail, kt, kb, vt, vtb, kit, ki4) = _mixer_in(
        x_prompt, jnp.zeros((b, POOL_PAD, D_POOL), F32), cos_p, sin_p, pos1_p, g1, w_all, bg, mixw, pscale,
        wpo, bt=1, lt=KEY_CHUNK, prompt=True)
    attn = _prompt_attention(qi, misc, q, ki4, kb, vtb, topk=topk_p)
    y_p, ctail = _output_stage(x_prompt, pa, gb, attn, jnp.zeros((b, CONV_PAD, c2), F32), wao, wout, g2,
                               wup, cw, cb, wdn, gf, bt=1, lt=256, ff_chunk=256)
    outs_p = (jnp.transpose(kt.reshape(b, N_HEADS, HEAD_DIM, l), (0, 3, 1, 2))[None],
              jnp.transpose(vt.reshape(b, N_HEADS, HEAD_DIM, l), (0, 3, 1, 2))[None],
              jnp.transpose(kit, (0, 2, 1))[None],
              ptail[:, 1:][None], ctail[:, CONV_PAD - 2:][None])

    pre_pool = jnp.concatenate([jnp.zeros((db, 1, D_POOL), F32), state_pool[li]], axis=1)
    (pa, gb, q, qi, misc, ptail, k, v, ki) = _mixer_in(
        x_sample, pre_pool, cos_s, sin_s, pos1_s, g1, w_all, bg, mixw, pscale, wpo,
        bt=sample_bt, lt=dl, prompt=False)
    keep = _sample_index(page_table, qi, misc, ki, cache_kit, li, topk=topk_s, pages_per_step=16)
    attn = _sample_attention(page_table, q, keep, k, v, cache_kt, cache_vt, li, pages_per_step=16)
    pre_conv = jnp.concatenate([jnp.zeros((db, CONV_PAD - 2, c2), F32), state_conv[li]], axis=1)
    y_s, ctail = _output_stage(x_sample, pa, gb, attn, pre_conv, wao, wout, g2, wup, cw, cb, wdn, gf,
                               bt=sample_bt, lt=dl, ff_chunk=256)
    outs_s = (k.reshape(db, dl, N_HEADS, HEAD_DIM)[None], v.reshape(db, dl, N_HEADS, HEAD_DIM)[None],
              ki[None], ptail[:, 1:][None], ctail[:, CONV_PAD - 2:][None])

    return (y_p, y_s, *outs_p, *outs_s)
```

```python
import functools
import math

import jax
import jax.numpy as jnp
from jax import lax
from jax.experimental import pallas as pl
from jax.experimental.pallas import tpu as pltpu

F32 = jnp.float32
BF16 = jnp.bfloat16
I32 = jnp.int32

LANES = 128
SUBLANES = 8
VMEM_LIMIT_BYTES = 56 * 1024 * 1024

N_HEADS = 8
HEAD_DIM = 64
D_ATTN = N_HEADS * HEAD_DIM
IDX_HEADS = 8
IDX_DIM = 64
D_POOL = 512
POOL_WINDOWS = (2, 4, 8, 16)
POOL_PAD = 16
CONV_PAD = 8
INDEX_TOPK = 256
Q_BLOCK = 512
KEY_CHUNK = 256
ROPE_THETA = 10000.0
RMS_EPS = 1e-6
MASK_BIAS = -1e30
INT_MIN = -(2 ** 31)
NEG_INF = float("-inf")

C_P, C_Q, C_K, C_V, C_QI = 0, 512, 1024, 1536, 2048
C_KI4 = 2560
C_IW = 2816
C_GA = 2944
C_GB = 3968
C_END = 4992


def _params():
    return pltpu.CompilerParams(dimension_semantics=("arbitrary", "arbitrary"),
                                vmem_limit_bytes=VMEM_LIMIT_BYTES)


def _rms(x, g):
    return x * lax.rsqrt(jnp.mean(x * x, axis=-1, keepdims=True) + RMS_EPS) * g


def _rope_blocks(z, cos, sin, first_half):
    outs = []
    for c in range(z.shape[1] // LANES):
        zb = z[:, c * LANES:(c + 1) * LANES]
        sw = jnp.where(first_half, pltpu.roll(zb, LANES - 32, 1), pltpu.roll(zb, 32, 1))
        outs.append(zb * cos + sw * sin)
    return outs[0] if len(outs) == 1 else jnp.concatenate(outs, axis=1)


def _bdot(a, b):
    return jnp.dot(a, b, preferred_element_type=F32)


def _dot_nt(a, b):
    return lax.dot_general(a, b, (((1,), (1,)), ((), ())), preferred_element_type=F32)


def _split(x):
    hi = x.astype(BF16)
    return hi, (x - hi.astype(F32)).astype(BF16)


def _mixer_in_kernel(x_ref, pre_ref, cos_ref, sin_ref, pos1_ref, g1_ref, w_ref, bg_ref, mixw_ref,
                     pscale_ref, wpo_ref, pa_ref, gb_ref, q_ref, qi_ref, misc_ref, tail_ref, *rest, prompt):
    carry_ref = rest[-1]
    j = pl.program_id(1)
    bt, lt, d = x_ref.shape
    m = bt * lt
    narrow = pa_ref.dtype

    x = x_ref[...].reshape(m, d)
    h = _rms(x, g1_ref[...]).astype(BF16)

    def rows(ref):
        t = ref[...]
        if bt == 1:
            return t
        return jnp.broadcast_to(t[None], (bt, lt, LANES)).reshape(m, LANES)

    cos, sin, pos1 = rows(cos_ref), rows(sin_ref), rows(pos1_ref)
    lane = lax.broadcasted_iota(I32, (m, LANES), 1)
    first_half = (lane % 64) < 32

    def proj(c0, c1):
        return _bdot(h, w_ref[:, c0:c1])

    p = proj(C_P, C_Q)

    @pl.when(j == 0)
    def _():
        carry_ref[...] = pre_ref[...]

    ext = jnp.concatenate([carry_ref[...], p.reshape(bt, lt, D_POOL)], axis=1)
    tail = ext[:, lt:, :]
    carry_ref[...] = tail
    tail_ref[...] = tail
    e2 = ext.reshape(bt * (POOL_PAD + lt), D_POOL)
    pools = []
    for g, w in enumerate(POOL_WINDOWS):
        s = e2[:, g * LANES:(g + 1) * LANES]
        sh = 1
        while sh < w:
            s = s + pltpu.roll(s, sh, 0)
            sh *= 2
        wsum = s.reshape(bt, POOL_PAD + lt, LANES)[:, POOL_PAD:, :].reshape(m, LANES)
        dlt = wsum / jnp.minimum(pos1, float(w)) - p[:, g * LANES:(g + 1) * LANES]
        pools.append(_bdot(dlt.astype(BF16), mixw_ref[g]))
    pool = jnp.concatenate(pools, axis=1) * pscale_ref[...]
    a = _bdot(pool.astype(BF16), wpo_ref[...])

    bg = bg_ref[...]
    ga = jax.nn.sigmoid(proj(C_GA, C_GB) + bg[:, :d])
    pa_ref[...] = (ga * a).astype(narrow).reshape(bt, lt, d)
    gb = jax.nn.sigmoid(proj(C_GB, C_END) + bg[:, d:])
    gb_ref[...] = gb.astype(narrow).reshape(bt, lt, d)

    q = _rope_blocks(proj(C_Q, C_K), cos, sin, first_half)
    q_ref[...] = q.astype(narrow).reshape(bt, lt, D_ATTN)
    qi = _rope_blocks(proj(C_QI, C_KI4), cos, sin, first_half)
    qi_ref[...] = qi.reshape(bt, lt, 512)
    misc_ref[...] = (proj(C_IW, C_GA) * (IDX_HEADS ** -0.5 * IDX_DIM ** -0.5)).reshape(bt, lt, LANES)
    k = _rope_blocks(proj(C_K, C_V), cos, sin, first_half)
    v = proj(C_V, C_QI)
    ki4 = _rope_blocks(proj(C_KI4, C_IW), cos, sin, first_half)
    if prompt:
        kt_ref, kb_ref, vt_ref, vtb_ref, kit_ref, ki4_ref = rest[:-1]
        kt_ref[0] = k.T
        kb_ref[...] = k.astype(BF16).reshape(bt, lt, D_ATTN)
        vt = v.T
        vt_ref[0] = vt
        vtb_ref[0, 0] = vt.astype(BF16)
        kit_ref[0] = ki4[:, :LANES].T[:IDX_DIM, :]
        hi, lo = _split(ki4)
        ki4_ref[...] = jnp.concatenate([hi[:, :LANES], lo[:, LANES:]], axis=1).reshape(bt, lt, 256)
    else:
        k_ref, v_ref, ki_ref = rest[:-1]
        k_ref[...] = k.reshape(bt, lt, D_ATTN)
        v_ref[...] = v.reshape(bt, lt, D_ATTN)
        ki_ref[...] = ki4[:, :IDX_DIM].reshape(bt, lt, IDX_DIM)


def _mixer_in(x, prefix16, cos, sin, pos1, g1, w_all, bg, mixw, pscale, wpo, *, bt, lt, prompt):
    b, l, d = x.shape
    nb, nj = b // bt, l // lt
    narrow = BF16 if prompt else F32
    tok = lambda c: pl.BlockSpec((bt, lt, c), lambda i, j: (i, j, 0))
    full = lambda a: pl.BlockSpec(a.shape, lambda i, j: (0,) * a.ndim)
    tab = pl.BlockSpec((lt, LANES), lambda i, j: (j, 0))
    pre = pl.BlockSpec((bt, POOL_PAD, D_POOL), lambda i, j: (i, 0, 0))
    sds = lambda c, dt: jax.ShapeDtypeStruct((b, l, c), dt)
    out_shape = [sds(d, narrow), sds(d, narrow), sds(D_ATTN, narrow), sds(512, F32), sds(LANES, F32),
                 jax.ShapeDtypeStruct((b, POOL_PAD, D_POOL), F32)]
    out_specs = [tok(d), tok(d), tok(D_ATTN), tok(512), tok(LANES), pre]
    if prompt:
        assert bt == 1
        tr = lambda c: pl.BlockSpec((1, c, lt), lambda i, j: (i, 0, j))
        trs = lambda c: jax.ShapeDtypeStruct((b, c, l), F32)
        out_shape += [trs(D_ATTN), sds(D_ATTN, BF16), trs(D_ATTN),
                      jax.ShapeDtypeStruct((b, nj, D_ATTN, lt), BF16), trs(IDX_DIM), sds(256, BF16)]
        out_specs += [tr(D_ATTN), tok(D_ATTN), tr(D_ATTN),
                      pl.BlockSpec((1, 1, D_ATTN, lt), lambda i, j: (i, j, 0, 0)), tr(IDX_DIM), tok(256)]
    else:
        out_shape += [sds(D_ATTN, F32), sds(D_ATTN, F32), sds(IDX_DIM, F32)]
        out_specs += [tok(D_ATTN), tok(D_ATTN), tok(IDX_DIM)]
    return pl.pallas_call(
        functools.partial(_mixer_in_kernel, prompt=prompt),
        grid=(nb, nj),
        in_specs=[tok(d), pre, tab, tab, tab,
                  full(g1), full(w_all), full(bg), full(mixw), full(pscale), full(wpo)],
        out_specs=tuple(out_specs),
        out_shape=tuple(out_shape),
        scratch_shapes=[pltpu.VMEM((bt, POOL_PAD, D_POOL), F32)],
        compiler_params=_params(),
        name="mixer_in",
    )(x, prefix16, cos, sin, pos1, g1, w_all, bg, mixw, pscale, wpo)


def _ordinal_to_float(u):
    key = u ^ INT_MIN
    return lax.bitcast_convert_type(key ^ ((key >> 31) & 0x7FFFFFFF), F32)


def _count(key_ref, n_chunks, chunk, pred):
    width = key_ref.shape[1]

    def body(c, acc):
        r0 = pl.multiple_of(c * chunk, chunk)
        kc = key_ref[pl.ds(r0, chunk), :]
        hit = jnp.where(pred(kc, r0), 1, 0).astype(I32)
        return acc + jnp.sum(hit.reshape(chunk // SUBLANES, SUBLANES, width), axis=0)

    acc = lax.fori_loop(0, n_chunks, body, jnp.zeros((SUBLANES, width), I32))
    return jnp.sum(acc, axis=0, keepdims=True)


def _topk_select(key_ref, n_chunks, chunk, topk, idx_bits):
    width = key_ref.shape[1]

    def bit_body(t, ans):
        cand = ans | jnp.left_shift(jnp.int32(1), 31 - t)
        ck = _ordinal_to_float(cand)
        cnt = _count(key_ref, n_chunks, chunk, lambda kc, r0: kc >= ck)
        return jnp.where(cnt >= topk, cand, ans)

    thr = _ordinal_to_float(lax.fori_loop(0, 32, bit_body, jnp.zeros((1, width), I32)))
    n_gt = _count(key_ref, n_chunks, chunk, lambda kc, r0: kc > thr)
    n_eq = _count(key_ref, n_chunks, chunk, lambda kc, r0: kc == thr)
    need = topk - n_gt
    big = jnp.full((1, width), 2 ** idx_bits, I32)

    def tie_search():
        def idx_body(t, xs):
            cand = xs | jnp.left_shift(jnp.int32(1), idx_bits - 1 - t)

            def pred(kc, r0):
                row = r0 + lax.broadcasted_iota(I32, kc.shape, 0)
                return (kc == thr) & (row < cand)

            cnt = _count(key_ref, n_chunks, chunk, pred)
            return jnp.where(cnt < need, cand, xs)

        return lax.fori_loop(0, idx_bits, idx_body, jnp.zeros((1, width), I32))

    has_tie = jnp.max(n_eq - need) > 0
    cut = lax.cond(has_tie, tie_search, lambda: big)
    return thr, cut


def _prompt_attn_kernel(qi_ref, misc_ref, q_ref, ki4_ref, kb_ref, vt_ref, o_ref,
                        key_ref, bias_ref, rhs_ref, qbd_ref, acc_ref, *, topk):
    i = pl.program_id(1)
    qb = q_ref.shape[1]
    lk = ki4_ref.shape[1]
    chunk = vt_ref.shape[3]
    n_pairs = N_HEADS // 2
    n_chunks = (i * qb + qb + chunk - 1) // chunk
    q_pos = i * qb + lax.broadcasted_iota(I32, (1, qb), 1)

    qhi, qlo = _split(qi_ref[0].T)
    iwt = misc_ref[0].T[:IDX_HEADS, :]

    def head_rhs(hh):
        a, b = qhi[hh * 64:(hh + 1) * 64], qlo[hh * 64:(hh + 1) * 64]
        return jnp.concatenate([a, b, a, b], axis=0)

    for jp in range(n_pairs):
        rhs_ref[jp] = jnp.concatenate([head_rhs(2 * jp), head_rhs(2 * jp + 1)], axis=1)

    def score_body(c, carry):
        r0 = pl.multiple_of(c * chunk, chunk)
        kc = ki4_ref[0, pl.ds(r0, chunk), :]
        sc = jnp.zeros((chunk, qb), F32)
        for jp in range(n_pairs):
            dots = jnp.maximum(_bdot(kc, rhs_ref[jp]), 0.0)
            sc = sc + dots[:, :qb] * iwt[2 * jp:2 * jp + 1, :] + dots[:, qb:] * iwt[2 * jp + 1:2 * jp + 2, :]
        row = r0 + lax.broadcasted_iota(I32, (chunk, qb), 0)
        key_ref[pl.ds(r0, chunk), :] = jnp.where(row <= q_pos, sc, NEG_INF)
        return carry

    lax.fori_loop(0, n_chunks, score_body, 0)

    keep_all = (jnp.full((1, qb), NEG_INF, F32), jnp.full((1, qb), lk, I32))

    def searched():
        thr, cut = _topk_select(key_ref, n_chunks, chunk, topk, int(math.log2(lk)))
        few = q_pos < topk
        return jnp.where(few, keep_all[0], thr), jnp.where(few, keep_all[1], cut)

    thr, cut = lax.cond((i * qb + qb) > topk, searched, lambda: keep_all)

    def bias_body(c, carry):
        r0 = pl.multiple_of(c * chunk, chunk)
        kc = key_ref[pl.ds(r0, chunk), :]
        row = r0 + lax.broadcasted_iota(I32, (chunk, qb), 0)
        keep = ((kc > thr) | ((kc == thr) & (row <= cut))) & (row <= q_pos)
        bias_ref[pl.ds(r0, chunk), :] = jnp.where(keep, 0.0, MASK_BIAS)
        return carry

    lax.fori_loop(0, n_chunks, bias_body, 0)

    qt = q_ref[0].astype(F32).T.astype(BF16)
    zero = jnp.zeros((HEAD_DIM, qb), BF16)
    for jp in range(n_pairs):
        h0, h1 = 2 * jp, 2 * jp + 1
        qbd_ref[jp] = jnp.concatenate(
            [jnp.concatenate([qt[h0 * 64:(h0 + 1) * 64], zero], axis=1),
             jnp.concatenate([zero, qt[h1 * 64:(h1 + 1) * 64]], axis=1)], axis=0)
    acc_ref[...] = jnp.zeros(acc_ref.shape, F32)

    def attn_body(c, carry):
        r0 = pl.multiple_of(c * chunk, chunk)
        bias = bias_ref[pl.ds(r0, chunk), :]
        bias2 = jnp.concatenate([bias, bias], axis=1)
        s_all = [_bdot(kb_ref[0, pl.ds(r0, chunk), jp * LANES:(jp + 1) * LANES], qbd_ref[jp])
                 for jp in range(n_pairs)]
        new, ps, corrs = [], [], []
        for jp in range(n_pairs):
            m_run, l_run = carry[jp]
            s = s_all[jp] + bias2
            m_new = jnp.maximum(m_run, jnp.max(s, axis=0, keepdims=True))
            p = jnp.exp(s - m_new)
            corr = jnp.exp(m_run - m_new)
            new.append((m_new, l_run * corr + jnp.sum(p, axis=0, keepdims=True)))
            ps.append(p.astype(BF16))
            corrs.append(corr)
        pv = [_bdot(vt_ref[0, c, jp * LANES:(jp + 1) * LANES, :], ps[jp]) for jp in range(n_pairs)]
        for jp in range(n_pairs):
            acc_ref[jp] = acc_ref[jp] * corrs[jp] + pv[jp]
        return tuple(new)

    init = tuple((jnp.full((1, 2 * qb), MASK_BIAS, F32), jnp.zeros((1, 2 * qb), F32))
                 for _ in range(n_pairs))
    fin = lax.fori_loop(0, n_chunks, attn_body, init)
    outs = []
    for jp in range(n_pairs):
        o = acc_ref[jp] / fin[jp][1]
        outs.append(o[:HEAD_DIM, :qb])
        outs.append(o[HEAD_DIM:, qb:])
    o_ref[0] = jnp.concatenate(outs, axis=0).T.astype(o_ref.dtype)


def _prompt_attention(qi, misc, q, ki4, kb, vt, *, topk):
    b, l, _ = q.shape
    nq = l // Q_BLOCK
    blk = lambda c: pl.BlockSpec((1, Q_BLOCK, c), lambda i, j: (i, j, 0))
    seq = lambda c: pl.BlockSpec((1, l, c), lambda i, j: (i, 0, 0))
    n_pairs = N_HEADS // 2
    return pl.pallas_call(
        functools.partial(_prompt_attn_kernel, topk=topk),
        grid=(b, nq),
        in_specs=[blk(512), blk(LANES), blk(D_ATTN), seq(256), seq(D_ATTN),
                  pl.BlockSpec((1,) + vt.shape[1:], lambda i, j: (i, 0, 0, 0))],
        out_specs=blk(D_ATTN),
        out_shape=jax.ShapeDtypeStruct((b, l, D_ATTN), BF16),
        scratch_shapes=[pltpu.VMEM((l, Q_BLOCK), F32), pltpu.VMEM((l, Q_BLOCK), F32),
                        pltpu.VMEM((n_pairs, 4 * IDX_DIM, 2 * Q_BLOCK), BF16),
                        pltpu.VMEM((n_pairs, 2 * HEAD_DIM, 2 * Q_BLOCK), BF16),
                        pltpu.VMEM((n_pairs, 2 * HEAD_DIM, 2 * Q_BLOCK), F32)],
        compiler_params=_params(),
        name="prompt_attention",
    )(qi, misc, q, ki4, kb, vt)


def _gelu_tanh(x):
    return 0.5 * x * (1.0 + jnp.tanh(math.sqrt(2.0 / math.pi) * (x + 0.044715 * (x * x * x))))


def _output_kernel(x_ref, pa_ref, gb_ref, at_ref, pre_ref, wao_ref, wout_ref, g2_ref, wup_ref, cw_ref,
                   cb_ref, wdn_ref, gf_ref, y_ref, tail_ref, carry_ref, act_ref, *, ff_chunk):
    j = pl.program_id(1)
    bt, lt, d = x_ref.shape
    m = bt * lt
    dff = wdn_ref.shape[0]

    x = x_ref[...].reshape(m, d)
    at = at_ref[...].reshape(m, D_ATTN).astype(BF16)
    mrg = (pa_ref[...].reshape(m, d).astype(F32)
           + gb_ref[...].reshape(m, d).astype(F32) * _bdot(at, wao_ref[...]))
    x1 = x + _bdot(mrg.astype(BF16), wout_ref[...])
    h2 = _rms(x1, g2_ref[...]).astype(BF16)

    @pl.when(j == 0)
    def _():
        carry_ref[...] = pre_ref[...]

    def conv(c0):
        u = _bdot(h2, wup_ref[:, c0:c0 + ff_chunk])
        ext = jnp.concatenate([carry_ref[:, :, c0:c0 + ff_chunk], u.reshape(bt, lt, ff_chunk)], axis=1)
        carry_ref[:, :, c0:c0 + ff_chunk] = ext[:, lt:, :]
        e2 = ext.reshape(bt * (CONV_PAD + lt), ff_chunk)

        def shifted(e):
            return e.reshape(bt, CONV_PAD + lt, ff_chunk)[:, CONV_PAD:, :].reshape(m, ff_chunk)

        cw = cw_ref[:, c0:c0 + ff_chunk]
        return (cb_ref[:, c0:c0 + ff_chunk] + cw[0:1] * shifted(pltpu.roll(e2, 2, 0))
                + cw[1:2] * shifted(pltpu.roll(e2, 1, 0)) + cw[2:3] * u)

    for cc in range(dff // ff_chunk):
        gate = conv(cc * ff_chunk)
        val = conv(dff + cc * ff_chunk)
        act_ref[:, cc * ff_chunk:(cc + 1) * ff_chunk] = (_gelu_tanh(gate) * val).astype(BF16)
    tail_ref[...] = carry_ref[...]
    y_ref[...] = _rms(x1 + _bdot(act_ref[...], wdn_ref[...]), gf_ref[...]).reshape(bt, lt, d)


def _output_stage(x, pa, gb, attn, prefix8, wao, wout, g2, wup, cw, cb, wdn, gf, *, bt, lt, ff_chunk):
    b, l, d = x.shape
    nb, nj = b // bt, l // lt
    c2 = wup.shape[1]
    tok = lambda c: pl.BlockSpec((bt, lt, c), lambda i, j: (i, j, 0))
    full = lambda a: pl.BlockSpec(a.shape, lambda i, j: (0,) * a.ndim)
    pre = pl.BlockSpec((bt, CONV_PAD, c2), lambda i, j: (i, 0, 0))
    return pl.pallas_call(
        functools.partial(_output_kernel, ff_chunk=ff_chunk),
        grid=(nb, nj),
        in_specs=[tok(d), tok(d), tok(d), tok(D_ATTN), pre, full(wao), full(wout), full(g2), full(wup),
                  full(cw), full(cb), full(wdn), full(gf)],
        out_specs=(tok(d), pre),
        out_shape=(jax.ShapeDtypeStruct((b, l, d), F32), jax.ShapeDtypeStruct((b, CONV_PAD, c2), F32)),
        scratch_shapes=[pltpu.VMEM((bt, CONV_PAD, c2), F32), pltpu.VMEM((bt * lt, c2 // 2), BF16)],
        compiler_params=_params(),
        name="output_stage",
    )(x, pa, gb, attn, prefix8, wao, wout, g2, wup, cw, cb, wdn, gf)


def _sample_index_kernel(pt_ref, qi_ref, misc_ref, kin_ref, *rest, topk, pages_per_step):
    page_refs = rest[:pages_per_step]
    keep_ref, key_ref = rest[pages_per_step], rest[pages_per_step + 1]
    g = pl.program_id(1)
    ng = pl.num_programs(1)
    nq = qi_ref.shape[1]
    page = page_refs[0].shape[3]
    n_pages = key_ref.shape[0] - 1
    past_len = n_pages * page

    qi = qi_ref[0]
    qrows = jnp.concatenate([qi[:, hh * IDX_DIM:(hh + 1) * IDX_DIM] for hh in range(IDX_HEADS)], axis=0)
    qhi, qlo = _split(qrows)
    iw = misc_ref[0][:, :IDX_HEADS]

    def scores(kt):
        khi, klo = _split(kt)
        dots = _bdot(qhi, khi) + _bdot(qhi, klo) + _bdot(qlo, khi) + _bdot(qlo, klo)
        dots = jnp.maximum(dots, 0.0)
        sc = jnp.zeros((nq, kt.shape[1]), F32)
        for hh in range(IDX_HEADS):
            sc = sc + dots[hh * nq:(hh + 1) * nq, :] * iw[:, hh:hh + 1]
        return sc

    step_scores = scores(jnp.concatenate([r[0, 0] for r in page_refs], axis=1))
    for pp in range(pages_per_step):
        key_ref[g * pages_per_step + pp] = step_scores[:, pp * page:(pp + 1) * page]

    @pl.when(g == ng - 1)
    def _():
        kin = jnp.concatenate([kin_ref[0], jnp.zeros((page - nq, IDX_DIM), F32)], axis=0)
        qrow2 = lax.broadcasted_iota(I32, (nq, page), 0)
        col2 = lax.broadcasted_iota(I32, (nq, page), 1)
        key_ref[n_pages] = jnp.where(col2 <= qrow2, scores(kin.T), NEG_INF)

        keys = key_ref[...]
        shape = keys.shape
        col = lax.broadcasted_iota(I32, shape, 0) * page + lax.broadcasted_iota(I32, shape, 2)
        qrow = lax.broadcasted_iota(I32, shape, 1)

        def count(pred):
            return jnp.sum(jnp.sum(jnp.where(pred, 1, 0).astype(I32), axis=0), axis=1, keepdims=True)

        def bit_body(t, ans):
            cand = ans | jnp.left_shift(jnp.int32(1), 31 - t)
            return jnp.where(count(keys >= _ordinal_to_float(cand)) >= topk, cand, ans)

        thr = _ordinal_to_float(lax.fori_loop(0, 32, bit_body, jnp.zeros((nq, 1), I32)))
        need = topk - count(keys > thr)
        eq = keys == thr
        idx_bits = int(math.ceil(math.log2(past_len + page)))

        def tie_search():
            def idx_body(t, xs):
                cand = xs | jnp.left_shift(jnp.int32(1), idx_bits - 1 - t)
                return jnp.where(count(eq & (col < cand)) < need, cand, xs)

            return lax.fori_loop(0, idx_bits, idx_body, jnp.zeros((nq, 1), I32))

        has_tie = jnp.max(count(eq) - need) > 0
        cut = lax.cond(has_tie, tie_search, lambda: jnp.full((nq, 1), 2 ** idx_bits, I32))
        keep = ((keys > thr) | (eq & (col <= cut))) & (col <= past_len + qrow)
        keep_ref[0] = jnp.where(keep, 1.0, 0.0)


def _sample_index(page_table, qi, misc, ki_new, cache_kidx_t, layer, *, topk, pages_per_step):
    b, nq, _ = qi.shape
    n_pages = page_table.shape[1]
    page = cache_kidx_t.shape[3]
    assert n_pages * page >= topk
    pages_per_step = min(pages_per_step, n_pages)
    ng = n_pages // pages_per_step

    def page_spec(pp):
        return pl.BlockSpec((1, 1, IDX_DIM, page),
                            lambda i, g, pt: (layer, pt[i * n_pages + g * pages_per_step + pp], 0, 0))

    seq = lambda c: pl.BlockSpec((1, nq, c), lambda i, g, pt: (i, 0, 0))
    grid_spec = pltpu.PrefetchScalarGridSpec(
        num_scalar_prefetch=1,
        grid=(b, ng),
        in_specs=[seq(512), seq(LANES), seq(IDX_DIM)] + [page_spec(pp) for pp in range(pages_per_step)],
        out_specs=pl.BlockSpec((1, n_pages + 1, nq, page), lambda i, g, pt: (i, 0, 0, 0)),
        scratch_shapes=[pltpu.VMEM((n_pages + 1, nq, page), F32)],
    )
    return pl.pallas_call(
        functools.partial(_sample_index_kernel, topk=topk, pages_per_step=pages_per_step),
        grid_spec=grid_spec,
        out_shape=jax.ShapeDtypeStruct((b, n_pages + 1, nq, page), F32),
        compiler_params=_params(),
        name="sample_index",
    )(page_table.reshape(-1), qi, misc, ki_new, *([cache_kidx_t] * pages_per_step))


def _sample_attn_kernel(pt_ref, q_ref, keep_ref, keepn_ref, kn_ref, vn_ref, *rest, pages_per_step):
    k_refs = rest[:pages_per_step]
    v_refs = rest[pages_per_step:2 * pages_per_step]
    o_ref, m_ref, l_ref, acc_ref = rest[2 * pages_per_step:]
    g = pl.program_id(1)
    ng = pl.num_programs(1)
    nq = q_ref.shape[1]
    page = k_refs[0].shape[4]

    q = q_ref[0].astype(BF16)
    qh = [q[:, hh * HEAD_DIM:(hh + 1) * HEAD_DIM] for hh in range(N_HEADS)]

    @pl.when(g == 0)
    def _():
        m_ref[...] = jnp.full(m_ref.shape, MASK_BIAS, F32)
        l_ref[...] = jnp.zeros(l_ref.shape, F32)
        acc_ref[...] = jnp.zeros(acc_ref.shape, F32)

    def update(score_h, out_h, keep):
        s = jnp.concatenate([score_h(hh) for hh in range(N_HEADS)], axis=0)
        s = jnp.where(jnp.concatenate([keep] * N_HEADS, axis=0) > 0.5, s, MASK_BIAS)
        m_old = m_ref[...]
        m_new = jnp.maximum(m_old, jnp.max(s, axis=1, keepdims=True))
        p = jnp.exp(s - m_new)
        corr = jnp.exp(m_old - m_new)
        l_ref[...] = l_ref[...] * corr + jnp.sum(p, axis=1, keepdims=True)
        pb = p.astype(BF16)
        pv = jnp.concatenate([out_h(hh, pb[hh * nq:(hh + 1) * nq]) for hh in range(N_HEADS)], axis=0)
        acc_ref[...] = acc_ref[...] * corr + pv
        m_ref[...] = m_new

    def head_cat(refs, hh):
        return jnp.concatenate([r[0, 0, hh] for r in refs], axis=1).astype(BF16)

    update(lambda hh: _bdot(qh[hh], head_cat(k_refs, hh)),
           lambda hh, p: _dot_nt(p, head_cat(v_refs, hh)),
           jnp.concatenate([keep_ref[0, pp] for pp in range(pages_per_step)], axis=1))

    @pl.when(g == ng - 1)
    def _():
        pad = jnp.zeros((page - nq, D_ATTN), F32)
        kn = jnp.concatenate([kn_ref[0], pad], axis=0).astype(BF16)
        vn = jnp.concatenate([vn_ref[0], pad], axis=0).astype(BF16)
        update(lambda hh: _dot_nt(qh[hh], kn[:, hh * HEAD_DIM:(hh + 1) * HEAD_DIM]),
               lambda hh, p: _bdot(p, vn[:, hh * HEAD_DIM:(hh + 1) * HEAD_DIM]), keepn_ref[0, 0])
        o = acc_ref[...] / l_ref[...]
        o_ref[0] = jnp.concatenate([o[hh * nq:(hh + 1) * nq, :] for hh in range(N_HEADS)], axis=1)


def _sample_attention(page_table, q, keep, k_new, v_new, cache_kt, cache_vt, layer, *, pages_per_step):
    b, nq, _ = q.shape
    n_pages = page_table.shape[1]
    page = cache_kt.shape[4]
    pages_per_step = min(pages_per_step, n_pages)
    ng = n_pages // pages_per_step

    def page_spec(pp):
        return pl.BlockSpec((1, 1, N_HEADS, HEAD_DIM, page),
                            lambda i, g, pt: (layer, pt[i * n_pages + g * pages_per_step + pp], 0, 0, 0))

    seq = lambda c: pl.BlockSpec((1, nq, c), lambda i, g, pt: (i, 0, 0))
    grid_spec = pltpu.PrefetchScalarGridSpec(
        num_scalar_prefetch=1,
        grid=(b, ng),
        in_specs=[seq(D_ATTN),
                  pl.BlockSpec((1, pages_per_step, nq, page), lambda i, g, pt: (i, g, 0, 0)),
                  pl.BlockSpec((1, 1, nq, page), lambda i, g, pt: (i, n_pages, 0, 0)),
                  seq(D_ATTN), seq(D_ATTN)] + [page_spec(pp) for pp in range(pages_per_step)] * 2,
        out_specs=seq(D_ATTN),
        scratch_shapes=[pltpu.VMEM((N_HEADS * nq, 1), F32), pltpu.VMEM((N_HEADS * nq, 1), F32),
                        pltpu.VMEM((N_HEADS * nq, HEAD_DIM), F32)],
    )
    return pl.pallas_call(
        functools.partial(_sample_attn_kernel, pages_per_step=pages_per_step),
        grid_spec=grid_spec,
        out_shape=jax.ShapeDtypeStruct((b, nq, D_ATTN), F32),
        compiler_params=_params(),
        name="sample_attention",
    )(page_table.reshape(-1), q, keep, keep, k_new, v_new,
      *([cache_kt] * pages_per_step), *([cache_vt] * pages_per_step))


def _rope_tables(pos):
    half = HEAD_DIM // 2
    freq = ROPE_THETA ** (-jnp.arange(half, dtype=F32) / half)
    ang = pos.astype(F32)[:, None] * freq[None, :]
    cos, sin = jnp.cos(ang), jnp.sin(ang)
    cos = jnp.tile(jnp.concatenate([cos, cos], axis=-1), (1, LANES // HEAD_DIM))
    sin = jnp.tile(jnp.concatenate([-sin, sin], axis=-1), (1, LANES // HEAD_DIM))
    pos1 = jnp.broadcast_to((pos + 1).astype(F32)[:, None], (pos.shape[0], LANES))
    return cos, sin, pos1


def _pack_w_in(w):
    d = w.shape[0]
    o_q, o_k, o_v, o_qi = D_POOL, D_POOL + D_ATTN, D_POOL + 2 * D_ATTN, D_POOL + 3 * D_ATTN
    o_ki = o_qi + IDX_HEADS * IDX_DIM
    o_iw = o_ki + IDX_DIM
    o_g = o_iw + IDX_HEADS
    parts = [w[:, :o_q], w[:, o_q:o_k] * (HEAD_DIM ** -0.5), w[:, o_k:o_qi],
             w[:, o_qi:o_ki], jnp.tile(w[:, o_ki:o_iw], (1, 4)),
             jnp.pad(w[:, o_iw:o_g], ((0, 0), (0, LANES - IDX_HEADS))), w[:, o_g:]]
    packed = jnp.concatenate(parts, axis=1)
    assert packed.shape == (d, C_END)
    return packed.astype(BF16)


def kernel(x_prompt, x_sample, cache_k, cache_v, cache_kidx, state_pool, state_conv, page_table, norm1_g,
           w_in, b_gate, pool_mix_w, pool_scale, w_pool_o, w_attn_o, w_out, norm2_g, w_up, conv_w, conv_b,
           w_down, normf_g):
    depth = w_in.shape[0]
    b, l, d = x_prompt.shape
    db, dl, _ = x_sample.shape
    n_pages = page_table.shape[1]
    page = cache_k.shape[2]
    past = n_pages * page
    c2 = w_up.shape[2]
    assert depth == 1 and d == 1024 and dl % SUBLANES == 0 and l % KEY_CHUNK == 0

    cos_p, sin_p, pos1_p = _rope_tables(jnp.arange(l))
    cos_s, sin_s, pos1_s = _rope_tables(past + jnp.arange(dl))
    topk_p = min(INDEX_TOPK, l // 4)
    topk_s = min(INDEX_TOPK, (past + dl) // 4)
    sample_bt = min(db, 256 // dl)

    cache_kt = jnp.transpose(cache_k, (0, 1, 3, 4, 2))
    cache_vt = jnp.transpose(cache_v, (0, 1, 3, 4, 2))
    cache_kit = jnp.transpose(cache_kidx, (0, 1, 3, 2))

    li = 0
    w_all = _pack_w_in(w_in[li])
    g1 = norm1_g[li][None]
    bg = b_gate[li][None]
    mixw = pool_mix_w[li].astype(BF16)
    pscale = pool_scale[li][None]
    wpo = w_pool_o[li].astype(BF16)
    wao = w_attn_o[li].astype(BF16)
    wout = w_out[li].astype(BF16)
    g2 = norm2_g[li][None]
    wup = w_up[li].astype(BF16)
    wdn = w_down[li].astype(BF16)
    cw = conv_w[li]
    cb = conv_b[li][None]
    gf = normf_g[None]

    (pa, gb, q, qi, misc, ptail, kt, kb, vt, vtb, kit, ki4) = _mixer_in(
        x_prompt, jnp.zeros((b, POOL_PAD, D_POOL), F32), cos_p, sin_p, pos1_p, g1, w_all, bg, mixw, pscale,
        wpo, bt=1, lt=KEY_CHUNK, prompt=True)
    attn = _prompt_attention(qi, misc, q, ki4, kb, vtb, topk=topk_p)
    y_p, ctail = _output_stage(x_prompt, pa, gb, attn, jnp.zeros((b, CONV_PAD, c2), F32), wao, wout, g2,
                               wup, cw, cb, wdn, gf, bt=1, lt=256, ff_chunk=256)
    outs_p = (jnp.transpose(kt.reshape(b, N_HEADS, HEAD_DIM, l), (0, 3, 1, 2))[None],
              jnp.transpose(vt.reshape(b, N_HEADS, HEAD_DIM, l), (0, 3, 1, 2))[None],
              jnp.transpose(kit, (0, 2, 1))[None],
              ptail[:, 1:][None], ctail[:, CONV_PAD - 2:][None])

    pre_pool = jnp.concatenate([jnp.zeros((db, 1, D_POOL), F32), state_pool[li]], axis=1)
    (pa, gb, q, qi, misc, ptail, k, v, ki) = _mixer_in(
        x_sample, pre_pool, cos_s, sin_s, pos1_s, g1, w_all, bg, mixw, pscale, wpo,
        bt=sample_bt, lt=dl, prompt=False)
    keep = _sample_index(page_table, qi, misc, ki, cache_kit, li, topk=topk_s, pages_per_step=16)
    attn = _sample_attention(page_table, q, keep, k, v, cache_kt, cache_vt, li, pages_per_step=16)
    pre_conv = jnp.concatenate([jnp.zeros((db, CONV_PAD - 2, c2), F32), state_conv[li]], axis=1)
    y_s, ctail = _output_stage(x_sample, pa, gb, attn, pre_conv, wao, wout, g2, wup, cw, cb, wdn, gf,
                               bt=sample_bt, lt=dl, ff_chunk=256)
    outs_s = (k.reshape(db, dl, N_HEADS, HEAD_DIM)[None], v.reshape(db, dl, N_HEADS, HEAD_DIM)[None],
              ki[None], ptail[:, 1:][None], ctail[:, CONV_PAD - 2:][None])

    return (y_p, y_s, *outs_p, *outs_s)
```

```python
import functools
import math

import jax
import jax.numpy as jnp
from jax import lax
from jax.experimental import pallas as pl
from jax.experimental.pallas import tpu as pltpu

F32 = jnp.float32
BF16 = jnp.bfloat16
I32 = jnp.int32

LANES = 128
SUBLANES = 8
VMEM_LIMIT_BYTES = 56 * 1024 * 1024

N_HEADS = 8
HEAD_DIM = 64
D_ATTN = N_HEADS * HEAD_DIM
IDX_HEADS = 8
IDX_DIM = 64
D_POOL = 512
POOL_WINDOWS = (2, 4, 8, 16)
POOL_PAD = 16
CONV_PAD = 8
INDEX_TOPK = 256
Q_BLOCK = 512
KEY_CHUNK = 256
ROPE_THETA = 10000.0
RMS_EPS = 1e-6
MASK_BIAS = -1e30
INT_MIN = -(2 ** 31)
NEG_INF = float("-inf")

C_P, C_Q, C_K, C_V, C_QI = 0, 512, 1024, 1536, 2048
C_KI4 = 2560
C_IW = 2816
C_GA = 2944
C_GB = 3968
C_END = 4992


def _params():
    return pltpu.CompilerParams(dimension_semantics=("arbitrary", "arbitrary"),
                                vmem_limit_bytes=VMEM_LIMIT_BYTES)


def _rms(x, g):
    return x * lax.rsqrt(jnp.mean(x * x, axis=-1, keepdims=True) + RMS_EPS) * g


def _rope_blocks(z, cos, sin, first_half):
    outs = []
    for c in range(z.shape[1] // LANES):
        zb = z[:, c * LANES:(c + 1) * LANES]
        sw = jnp.where(first_half, pltpu.roll(zb, LANES - 32, 1), pltpu.roll(zb, 32, 1))
        outs.append(zb * cos + sw * sin)
    return outs[0] if len(outs) == 1 else jnp.concatenate(outs, axis=1)


def _bdot(a, b):
    return jnp.dot(a, b, preferred_element_type=F32)


def _dot_nt(a, b):
    return lax.dot_general(a, b, (((1,), (1,)), ((), ())), preferred_element_type=F32)


def _split(x):
    hi = x.astype(BF16)
    return hi, (x - hi.astype(F32)).astype(BF16)


def _mixer_in_kernel(x_ref, pre_ref, cos_ref, sin_ref, pos1_ref, g1_ref, w_ref, bg_ref, mixw_ref,
                     pscale_ref, wpo_ref, pa_ref, gb_ref, q_ref, qi_ref, misc_ref, tail_ref, *rest, prompt):
    carry_ref = rest[-1]
    j = pl.program_id(1)
    bt, lt, d = x_ref.shape
    m = bt * lt
    narrow = pa_ref.dtype

    x = x_ref[...].reshape(m, d)
    h = _rms(x, g1_ref[...]).astype(BF16)

    def rows(ref):
        t = ref[...]
        if bt == 1:
            return t
        return jnp.broadcast_to(t[None], (bt, lt, LANES)).reshape(m, LANES)

    cos, sin, pos1 = rows(cos_ref), rows(sin_ref), rows(pos1_ref)
    lane = lax.broadcasted_iota(I32, (m, LANES), 1)
    first_half = (lane % 64) < 32

    def proj(c0, c1):
        return _bdot(h, w_ref[:, c0:c1])

    p = proj(C_P, C_Q)

    @pl.when(j == 0)
    def _():
        carry_ref[...] = pre_ref[...]

    ext = jnp.concatenate([carry_ref[...], p.reshape(bt, lt, D_POOL)], axis=1)
    tail = ext[:, lt:, :]
    carry_ref[...] = tail
    tail_ref[...] = tail
    e2 = ext.reshape(bt * (POOL_PAD + lt), D_POOL)
    pools = []
    for g, w in enumerate(POOL_WINDOWS):
        s = e2[:, g * LANES:(g + 1) * LANES]
        sh = 1
        while sh < w:
            s = s + pltpu.roll(s, sh, 0)
            sh *= 2
        wsum = s.reshape(bt, POOL_PAD + lt, LANES)[:, POOL_PAD:, :].reshape(m, LANES)
        dlt = wsum / jnp.minimum(pos1, float(w)) - p[:, g * LANES:(g + 1) * LANES]
        pools.append(_bdot(dlt.astype(BF16), mixw_ref[g]))
    pool = jnp.concatenate(pools, axis=1) * pscale_ref[...]
    a = _bdot(pool.astype(BF16), wpo_ref[...])

    bg = bg_ref[...]
    ga = jax.nn.sigmoid(proj(C_GA, C_GB) + bg[:, :d])
    pa_ref[...] = (ga * a).astype(narrow).reshape(bt, lt, d)
    gb = jax.nn.sigmoid(proj(C_GB, C_END) + bg[:, d:])
    gb_ref[...] = gb.astype(narrow).reshape(bt, lt, d)

    q = _rope_blocks(proj(C_Q, C_K), cos, sin, first_half)
    q_ref[...] = q.astype(narrow).reshape(bt, lt, D_ATTN)
    qi = _rope_blocks(proj(C_QI, C_KI4), cos, sin, first_half)
    qi_ref[...] = qi.reshape(bt, lt, 512)
    misc_ref[...] = (proj(C_IW, C_GA) * (IDX_HEADS ** -0.5 * IDX_DIM ** -0.5)).reshape(bt, lt, LANES)
    k = _rope_blocks(proj(C_K, C_V), cos, sin, first_half)
    v = proj(C_V, C_QI)
    ki4 = _rope_blocks(proj(C_KI4, C_IW), cos, sin, first_half)
    if prompt:
        kt_ref, kb_ref, vt_ref, vtb_ref, kit_ref, ki4_ref = rest[:-1]
        kt_ref[0] = k.T
        kb_ref[...] = k.astype(BF16).reshape(bt, lt, D_ATTN)
        vt = v.T
        vt_ref[0] = vt
        vtb_ref[0, 0] = vt.astype(BF16)
        kit_ref[0] = ki4[:, :LANES].T[:IDX_DIM, :]
        hi, lo = _split(ki4)
        ki4_ref[...] = jnp.concatenate([hi[:, :LANES], lo[:, LANES:]], axis=1).reshape(bt, lt, 256)
    else:
        k_ref, v_ref, ki_ref = rest[:-1]
        k_ref[...] = k.reshape(bt, lt, D_ATTN)
        v_ref[...] = v.reshape(bt, lt, D_ATTN)
        ki_ref[...] = ki4[:, :IDX_DIM].reshape(bt, lt, IDX_DIM)


def _mixer_in(x, prefix16, cos, sin, pos1, g1, w_all, bg, mixw, pscale, wpo, *, bt, lt, prompt):
    b, l, d = x.shape
    nb, nj = b // bt, l // lt
    narrow = BF16 if prompt else F32
    tok = lambda c: pl.BlockSpec((bt, lt, c), lambda i, j: (i, j, 0))
    full = lambda a: pl.BlockSpec(a.shape, lambda i, j: (0,) * a.ndim)
    tab = pl.BlockSpec((lt, LANES), lambda i, j: (j, 0))
    pre = pl.BlockSpec((bt, POOL_PAD, D_POOL), lambda i, j: (i, 0, 0))
    sds = lambda c, dt: jax.ShapeDtypeStruct((b, l, c), dt)
    out_shape = [sds(d, narrow), sds(d, narrow), sds(D_ATTN, narrow), sds(512, F32), sds(LANES, F32),
                 jax.ShapeDtypeStruct((b, POOL_PAD, D_POOL), F32)]
    out_specs = [tok(d), tok(d), tok(D_ATTN), tok(512), tok(LANES), pre]
    if prompt:
        assert bt == 1
        tr = lambda c: pl.BlockSpec((1, c, lt), lambda i, j: (i, 0, j))
        trs = lambda c: jax.ShapeDtypeStruct((b, c, l), F32)
        out_shape += [trs(D_ATTN), sds(D_ATTN, BF16), trs(D_ATTN),
                      jax.ShapeDtypeStruct((b, nj, D_ATTN, lt), BF16), trs(IDX_DIM), sds(256, BF16)]
        out_specs += [tr(D_ATTN), tok(D_ATTN), tr(D_ATTN),
                      pl.BlockSpec((1, 1, D_ATTN, lt), lambda i, j: (i, j, 0, 0)), tr(IDX_DIM), tok(256)]
    else:
        out_shape += [sds(D_ATTN, F32), sds(D_ATTN, F32), sds(IDX_DIM, F32)]
        out_specs += [tok(D_ATTN), tok(D_ATTN), tok(IDX_DIM)]
    return pl.pallas_call(
        functools.partial(_mixer_in_kernel, prompt=prompt),
        grid=(nb, nj),
        in_specs=[tok(d), pre, tab, tab, tab,
                  full(g1), full(w_all), full(bg), full(mixw), full(pscale), full(wpo)],
        out_specs=tuple(out_specs),
        out_shape=tuple(out_shape),
        scratch_shapes=[pltpu.VMEM((bt, POOL_PAD, D_POOL), F32)],
        compiler_params=_params(),
        name="mixer_in",
    )(x, prefix16, cos, sin, pos1, g1, w_all, bg, mixw, pscale, wpo)


def _ordinal_to_float(u):
    key = u ^ INT_MIN
    return lax.bitcast_convert_type(key ^ ((key >> 31) & 0x7FFFFFFF), F32)


def _count(key_ref, n_chunks, chunk, pred):
    width = key_ref.shape[1]

    def body(c, acc):
        r0 = pl.multiple_of(c * chunk, chunk)
        kc = key_ref[pl.ds(r0, chunk), :]
        hit = jnp.where(pred(kc, r0), 1, 0).astype(I32)
        return acc + jnp.sum(hit.reshape(chunk // SUBLANES, SUBLANES, width), axis=0)

    acc = lax.fori_loop(0, n_chunks, body, jnp.zeros((SUBLANES, width), I32))
    return jnp.sum(acc, axis=0, keepdims=True)


def _topk_select(key_ref, n_chunks, chunk, topk, idx_bits):
    width = key_ref.shape[1]

    def bit_body(t, ans):
        cand = ans | jnp.left_shift(jnp.int32(1), 31 - t)
        ck = _ordinal_to_float(cand)
        cnt = _count(key_ref, n_chunks, chunk, lambda kc, r0: kc >= ck)
        return jnp.where(cnt >= topk, cand, ans)

    thr = _ordinal_to_float(lax.fori_loop(0, 32, bit_body, jnp.zeros((1, width), I32)))
    n_gt = _count(key_ref, n_chunks, chunk, lambda kc, r0: kc > thr)
    n_eq = _count(key_ref, n_chunks, chunk, lambda kc, r0: kc == thr)
    need = topk - n_gt
    big = jnp.full((1, width), 2 ** idx_bits, I32)

    def tie_search():
        def idx_body(t, xs):
            cand = xs | jnp.left_shift(jnp.int32(1), idx_bits - 1 - t)

            def pred(kc, r0):
                row = r0 + lax.broadcasted_iota(I32, kc.shape, 0)
                return (kc == thr) & (row < cand)

            cnt = _count(key_ref, n_chunks, chunk, pred)
            return jnp.where(cnt < need, cand, xs)

        return lax.fori_loop(0, idx_bits, idx_body, jnp.zeros((1, width), I32))

    has_tie = jnp.max(n_eq - need) > 0
    cut = lax.cond(has_tie, tie_search, lambda: big)
    return thr, cut


def _prompt_attn_kernel(qi_ref, misc_ref, q_ref, ki4_ref, kb_ref, vt_ref, o_ref,
                        key_ref, bias_ref, rhs_ref, qbd_ref, acc_ref, *, topk):
    i = pl.program_id(1)
    qb = q_ref.shape[1]
    lk = ki4_ref.shape[1]
    chunk = vt_ref.shape[3]
    n_pairs = N_HEADS // 2
    n_chunks = (i * qb + qb + chunk - 1) // chunk
    q_pos = i * qb + lax.broadcasted_iota(I32, (1, qb), 1)

    qhi, qlo = _split(qi_ref[0].T)
    iwt = misc_ref[0].T[:IDX_HEADS, :]

    def head_rhs(hh):
        a, b = qhi[hh * 64:(hh + 1) * 64], qlo[hh * 64:(hh + 1) * 64]
        return jnp.concatenate([a, b, a, b], axis=0)

    for jp in range(n_pairs):
        rhs_ref[jp] = jnp.concatenate([head_rhs(2 * jp), head_rhs(2 * jp + 1)], axis=1)

    def score_body(c, carry):
        r0 = pl.multiple_of(c * chunk, chunk)
        kc = ki4_ref[0, pl.ds(r0, chunk), :]
        sc = jnp.zeros((chunk, qb), F32)
        for jp in range(n_pairs):
            dots = jnp.maximum(_bdot(kc, rhs_ref[jp]), 0.0)
            sc = sc + dots[:, :qb] * iwt[2 * jp:2 * jp + 1, :] + dots[:, qb:] * iwt[2 * jp + 1:2 * jp + 2, :]
        row = r0 + lax.broadcasted_iota(I32, (chunk, qb), 0)
        key_ref[pl.ds(r0, chunk), :] = jnp.where(row <= q_pos, sc, NEG_INF)
        return carry

    lax.fori_loop(0, n_chunks, score_body, 0)

    keep_all = (jnp.full((1, qb), NEG_INF, F32), jnp.full((1, qb), lk, I32))

    def searched():
        thr, cut = _topk_select(key_ref, n_chunks, chunk, topk, int(math.log2(lk)))
        few = q_pos < topk
        return jnp.where(few, keep_all[0], thr), jnp.where(few, keep_all[1], cut)

    thr, cut = lax.cond((i * qb + qb) > topk, searched, lambda: keep_all)

    def bias_body(c, carry):
        r0 = pl.multiple_of(c * chunk, chunk)
        kc = key_ref[pl.ds(r0, chunk), :]
        row = r0 + lax.broadcasted_iota(I32, (chunk, qb), 0)
        keep = ((kc > thr) | ((kc == thr) & (row <= cut))) & (row <= q_pos)
        bias_ref[pl.ds(r0, chunk), :] = jnp.where(keep, 0.0, MASK_BIAS)
        return carry

    lax.fori_loop(0, n_chunks, bias_body, 0)

    qt = q_ref[0].astype(F32).T.astype(BF16)
    zero = jnp.zeros((HEAD_DIM, qb), BF16)
    for jp in range(n_pairs):
        h0, h1 = 2 * jp, 2 * jp + 1
        qbd_ref[jp] = jnp.concatenate(
            [jnp.concatenate([qt[h0 * 64:(h0 + 1) * 64], zero], axis=1),
             jnp.concatenate([zero, qt[h1 * 64:(h1 + 1) * 64]], axis=1)], axis=0)
    acc_ref[...] = jnp.zeros(acc_ref.shape, F32)

    def attn_body(c, carry):
        r0 = pl.multiple_of(c * chunk, chunk)
        bias = bias_ref[pl.ds(r0, chunk), :]
        bias2 = jnp.concatenate([bias, bias], axis=1)
        s_all = [_bdot(kb_ref[0, pl.ds(r0, chunk), jp * LANES:(jp + 1) * LANES], qbd_ref[jp])
                 for jp in range(n_pairs)]
        new, ps, corrs = [], [], []
        for jp in range(n_pairs):
            m_run, l_run = carry[jp]
            s = s_all[jp] + bias2
            m_new = jnp.maximum(m_run, jnp.max(s, axis=0, keepdims=True))
            p = jnp.exp(s - m_new)
            corr = jnp.exp(m_run - m_new)
            new.append((m_new, l_run * corr + jnp.sum(p, axis=0, keepdims=True)))
            ps.append(p.astype(BF16))
            corrs.append(corr)
        pv = [_bdot(vt_ref[0, c, jp * LANES:(jp + 1) * LANES, :], ps[jp]) for jp in range(n_pairs)]
        for jp in range(n_pairs):
            acc_ref[jp] = acc_ref[jp] * corrs[jp] + pv[jp]
        return tuple(new)

    init = tuple((jnp.full((1, 2 * qb), MASK_BIAS, F32), jnp.zeros((1, 2 * qb), F32))
                 for _ in range(n_pairs))
    fin = lax.fori_loop(0, n_chunks, attn_body, init)
    outs = []
    for jp in range(n_pairs):
        o = acc_ref[jp] / fin[jp][1]
        outs.append(o[:HEAD_DIM, :qb])
        outs.append(o[HEAD_DIM:, qb:])
    o_ref[0] = jnp.concatenate(outs, axis=0).T.astype(o_ref.dtype)


def _prompt_attention(qi, misc, q, ki4, kb, vt, *, topk):
    b, l, _ = q.shape
    nq = l // Q_BLOCK
    blk = lambda c: pl.BlockSpec((1, Q_BLOCK, c), lambda i, j: (i, j, 0))
    seq = lambda c: pl.BlockSpec((1, l, c), lambda i, j: (i, 0, 0))
    n_pairs = N_HEADS // 2
    return pl.pallas_call(
        functools.partial(_prompt_attn_kernel, topk=topk),
        grid=(b, nq),
        in_specs=[blk(512), blk(LANES), blk(D_ATTN), seq(256), seq(D_ATTN),
                  pl.BlockSpec((1,) + vt.shape[1:], lambda i, j: (i, 0, 0, 0))],
        out_specs=blk(D_ATTN),
        out_shape=jax.ShapeDtypeStruct((b, l, D_ATTN), BF16),
        scratch_shapes=[pltpu.VMEM((l, Q_BLOCK), F32), pltpu.VMEM((l, Q_BLOCK), F32),
                        pltpu.VMEM((n_pairs, 4 * IDX_DIM, 2 * Q_BLOCK), BF16),
                        pltpu.VMEM((n_pairs, 2 * HEAD_DIM, 2 * Q_BLOCK), BF16),
                        pltpu.VMEM((n_pairs, 2 * HEAD_DIM, 2 * Q_BLOCK), F32)],
        compiler_params=_params(),
        name="prompt_attention",
    )(qi, misc, q, ki4, kb, vt)


def _gelu_tanh(x):
    return 0.5 * x * (1.0 + jnp.tanh(math.sqrt(2.0 / math.pi) * (x + 0.044715 * (x * x * x))))


def _output_kernel(x_ref, pa_ref, gb_ref, at_ref, pre_ref, wao_ref, wout_ref, g2_ref, wup_ref, cw_ref,
                   cb_ref, wdn_ref, gf_ref, y_ref, tail_ref, carry_ref, act_ref, *, ff_chunk):
    j = pl.program_id(1)
    bt, lt, d = x_ref.shape
    m = bt * lt
    dff = wdn_ref.shape[0]

    x = x_ref[...].reshape(m, d)
    at = at_ref[...].reshape(m, D_ATTN).astype(BF16)
    mrg = (pa_ref[...].reshape(m, d).astype(F32)
           + gb_ref[...].reshape(m, d).astype(F32) * _bdot(at, wao_ref[...]))
    x1 = x + _bdot(mrg.astype(BF16), wout_ref[...])
    h2 = _rms(x1, g2_ref[...]).astype(BF16)

    @pl.when(j == 0)
    def _():
        carry_ref[...] = pre_ref[...]

    def conv(c0):
        u = _bdot(h2, wup_ref[:, c0:c0 + ff_chunk])
        ext = jnp.concatenate([carry_ref[:, :, c0:c0 + ff_chunk], u.reshape(bt, lt, ff_chunk)], axis=1)
        carry_ref[:, :, c0:c0 + ff_chunk] = ext[:, lt:, :]
        e2 = ext.reshape(bt * (CONV_PAD + lt), ff_chunk)

        def shifted(e):
            return e.reshape(bt, CONV_PAD + lt, ff_chunk)[:, CONV_PAD:, :].reshape(m, ff_chunk)

        cw = cw_ref[:, c0:c0 + ff_chunk]
        return (cb_ref[:, c0:c0 + ff_chunk] + cw[0:1] * shifted(pltpu.roll(e2, 2, 0))
                + cw[1:2] * shifted(pltpu.roll(e2, 1, 0)) + cw[2:3] * u)

    for cc in range(dff // ff_chunk):
        gate = conv(cc * ff_chunk)
        val = conv(dff + cc * ff_chunk)
        act_ref[:, cc * ff_chunk:(cc + 1) * ff_chunk] = (_gelu_tanh(gate) * val).astype(BF16)
    tail_ref[...] = carry_ref[...]
    y_ref[...] = _rms(x1 + _bdot(act_ref[...], wdn_ref[...]), gf_ref[...]).reshape(bt, lt, d)


def _output_stage(x, pa, gb, attn, prefix8, wao, wout, g2, wup, cw, cb, wdn, gf, *, bt, lt, ff_chunk):
    b, l, d = x.shape
    nb, nj = b // bt, l // lt
    c2 = wup.shape[1]
    tok = lambda c: pl.BlockSpec((bt, lt, c), lambda i, j: (i, j, 0))
    full = lambda a: pl.BlockSpec(a.shape, lambda i, j: (0,) * a.ndim)
    pre = pl.BlockSpec((bt, CONV_PAD, c2), lambda i, j: (i, 0, 0))
    return pl.pallas_call(
        functools.partial(_output_kernel, ff_chunk=ff_chunk),
        grid=(nb, nj),
        in_specs=[tok(d), tok(d), tok(d), tok(D_ATTN), pre, full(wao), full(wout), full(g2), full(wup),
                  full(cw), full(cb), full(wdn), full(gf)],
        out_specs=(tok(d), pre),
        out_shape=(jax.ShapeDtypeStruct((b, l, d), F32), jax.ShapeDtypeStruct((b, CONV_PAD, c2), F32)),
        scratch_shapes=[pltpu.VMEM((bt, CONV_PAD, c2), F32), pltpu.VMEM((bt * lt, c2 // 2), BF16)],
        compiler_params=_params(),
        name="output_stage",
    )(x, pa, gb, attn, prefix8, wao, wout, g2, wup, cw, cb, wdn, gf)


def _sample_index_kernel(pt_ref, qi_ref, misc_ref, kin_ref, *rest, topk, pages_per_step):
    page_refs = rest[:pages_per_step]
    keep_ref, key_ref = rest[pages_per_step], rest[pages_per_step + 1]
    g = pl.program_id(1)
    ng = pl.num_programs(1)
    nq = qi_ref.shape[1]
    page = page_refs[0].shape[3]
    n_pages = key_ref.shape[0] - 1
    past_len = n_pages * page

    qi = qi_ref[0]
    qrows = jnp.concatenate([qi[:, hh * IDX_DIM:(hh + 1) * IDX_DIM] for hh in range(IDX_HEADS)], axis=0)
    qhi, qlo = _split(qrows)
    iw = misc_ref[0][:, :IDX_HEADS]

    def scores(kt):
        khi, klo = _split(kt)
        dots = _bdot(qhi, khi) + _bdot(qhi, klo) + _bdot(qlo, khi) + _bdot(qlo, klo)
        dots = jnp.maximum(dots, 0.0)
        sc = jnp.zeros((nq, kt.shape[1]), F32)
        for hh in range(IDX_HEADS):
            sc = sc + dots[hh * nq:(hh + 1) * nq, :] * iw[:, hh:hh + 1]
        return sc

    step_scores = scores(jnp.concatenate([r[0, 0] for r in page_refs], axis=1))
    for pp in range(pages_per_step):
        key_ref[g * pages_per_step + pp] = step_scores[:, pp * page:(pp + 1) * page]

    @pl.when(g == ng - 1)
    def _():
        kin = jnp.concatenate([kin_ref[0], jnp.zeros((page - nq, IDX_DIM), F32)], axis=0)
        qrow2 = lax.broadcasted_iota(I32, (nq, page), 0)
        col2 = lax.broadcasted_iota(I32, (nq, page), 1)
        key_ref[n_pages] = jnp.where(col2 <= qrow2, scores(kin.T), NEG_INF)

        keys = key_ref[...]
        shape = keys.shape
        col = lax.broadcasted_iota(I32, shape, 0) * page + lax.broadcasted_iota(I32, shape, 2)
        qrow = lax.broadcasted_iota(I32, shape, 1)

        def count(pred):
            return jnp.sum(jnp.sum(jnp.where(pred, 1, 0).astype(I32), axis=0), axis=1, keepdims=True)

        def enough(cand):
            return count(keys >= _ordinal_to_float(cand)) >= topk

        def bit_body(t, ans):
            hi = jnp.left_shift(jnp.int32(1), 31 - 2 * t)
            lo = jnp.left_shift(jnp.int32(1), 30 - 2 * t)
            c1, c2 = ans | lo, ans | hi
            c3 = c2 | lo
            return jnp.where(enough(c2), jnp.where(enough(c3), c3, c2), jnp.where(enough(c1), c1, ans))

        thr = _ordinal_to_float(lax.fori_loop(0, 16, bit_body, jnp.zeros((nq, 1), I32)))
        need = topk - count(keys > thr)
        eq = keys == thr
        idx_bits = int(math.ceil(math.log2(past_len + page)))

        def tie_search():
            def idx_body(t, xs):
                cand = xs | jnp.left_shift(jnp.int32(1), idx_bits - 1 - t)
                return jnp.where(count(eq & (col < cand)) < need, cand, xs)

            return lax.fori_loop(0, idx_bits, idx_body, jnp.zeros((nq, 1), I32))

        has_tie = jnp.max(count(eq) - need) > 0
        cut = lax.cond(has_tie, tie_search, lambda: jnp.full((nq, 1), 2 ** idx_bits, I32))
        keep = ((keys > thr) | (eq & (col <= cut))) & (col <= past_len + qrow)
        keep_ref[0] = jnp.where(keep, 1.0, 0.0)


def _sample_index(page_table, qi, misc, ki_new, cache_kidx_t, layer, *, topk, pages_per_step):
    b, nq, _ = qi.shape
    n_pages = page_table.shape[1]
    page = cache_kidx_t.shape[3]
    assert n_pages * page >= topk
    pages_per_step = min(pages_per_step, n_pages)
    ng = n_pages // pages_per_step

    def page_spec(pp):
        return pl.BlockSpec((1, 1, IDX_DIM, page),
                            lambda i, g, pt: (layer, pt[i * n_pages + g * pages_per_step + pp], 0, 0))

    seq = lambda c: pl.BlockSpec((1, nq, c), lambda i, g, pt: (i, 0, 0))
    grid_spec = pltpu.PrefetchScalarGridSpec(
        num_scalar_prefetch=1,
        grid=(b, ng),
        in_specs=[seq(512), seq(LANES), seq(IDX_DIM)] + [page_spec(pp) for pp in range(pages_per_step)],
        out_specs=pl.BlockSpec((1, n_pages + 1, nq, page), lambda i, g, pt: (i, 0, 0, 0)),
        scratch_shapes=[pltpu.VMEM((n_pages + 1, nq, page), F32)],
    )
    return pl.pallas_call(
        functools.partial(_sample_index_kernel, topk=topk, pages_per_step=pages_per_step),
        grid_spec=grid_spec,
        out_shape=jax.ShapeDtypeStruct((b, n_pages + 1, nq, page), F32),
        compiler_params=_params(),
        name="sample_index",
    )(page_table.reshape(-1), qi, misc, ki_new, *([cache_kidx_t] * pages_per_step))


def _sample_attn_kernel(pt_ref, q_ref, keep_ref, keepn_ref, kn_ref, vn_ref, *rest, pages_per_step):
    k_refs = rest[:pages_per_step]
    v_refs = rest[pages_per_step:2 * pages_per_step]
    o_ref, m_ref, l_ref, acc_ref = rest[2 * pages_per_step:]
    g = pl.program_id(1)
    ng = pl.num_programs(1)
    nq = q_ref.shape[1]
    page = k_refs[0].shape[4]

    q = q_ref[0].astype(BF16)
    qh = [q[:, hh * HEAD_DIM:(hh + 1) * HEAD_DIM] for hh in range(N_HEADS)]

    @pl.when(g == 0)
    def _():
        m_ref[...] = jnp.full(m_ref.shape, MASK_BIAS, F32)
        l_ref[...] = jnp.zeros(l_ref.shape, F32)
        acc_ref[...] = jnp.zeros(acc_ref.shape, F32)

    def update(score_h, out_h, keep):
        s = jnp.concatenate([score_h(hh) for hh in range(N_HEADS)], axis=0)
        s = jnp.where(jnp.concatenate([keep] * N_HEADS, axis=0) > 0.5, s, MASK_BIAS)
        m_old = m_ref[...]
        m_new = jnp.maximum(m_old, jnp.max(s, axis=1, keepdims=True))
        p = jnp.exp(s - m_new)
        corr = jnp.exp(m_old - m_new)
        l_ref[...] = l_ref[...] * corr + jnp.sum(p, axis=1, keepdims=True)
        pb = p.astype(BF16)
        pv = jnp.concatenate([out_h(hh, pb[hh * nq:(hh + 1) * nq]) for hh in range(N_HEADS)], axis=0)
        acc_ref[...] = acc_ref[...] * corr + pv
        m_ref[...] = m_new

    def head_cat(refs, hh):
        return jnp.concatenate([r[0, 0, hh] for r in refs], axis=1).astype(BF16)

    update(lambda hh: _bdot(qh[hh], head_cat(k_refs, hh)),
           lambda hh, p: _dot_nt(p, head_cat(v_refs, hh)),
           jnp.concatenate([keep_ref[0, pp] for pp in range(pages_per_step)], axis=1))

    @pl.when(g == ng - 1)
    def _():
        pad = jnp.zeros((page - nq, D_ATTN), F32)
        kn = jnp.concatenate([kn_ref[0], pad], axis=0).astype(BF16)
        vn = jnp.concatenate([vn_ref[0], pad], axis=0).astype(BF16)
        update(lambda hh: _dot_nt(qh[hh], kn[:, hh * HEAD_DIM:(hh + 1) * HEAD_DIM]),
               lambda hh, p: _bdot(p, vn[:, hh * HEAD_DIM:(hh + 1) * HEAD_DIM]), keepn_ref[0, 0])
        o = acc_ref[...] / l_ref[...]
        o_ref[0] = jnp.concatenate([o[hh * nq:(hh + 1) * nq, :] for hh in range(N_HEADS)], axis=1)


def _sample_attention(page_table, q, keep, k_new, v_new, cache_kt, cache_vt, layer, *, pages_per_step):
    b, nq, _ = q.shape
    n_pages = page_table.shape[1]
    page = cache_kt.shape[4]
    pages_per_step = min(pages_per_step, n_pages)
    ng = n_pages // pages_per_step

    def page_spec(pp):
        return pl.BlockSpec((1, 1, N_HEADS, HEAD_DIM, page),
                            lambda i, g, pt: (layer, pt[i * n_pages + g * pages_per_step + pp], 0, 0, 0))

    seq = lambda c: pl.BlockSpec((1, nq, c), lambda i, g, pt: (i, 0, 0))
    grid_spec = pltpu.PrefetchScalarGridSpec(
        num_scalar_prefetch=1,
        grid=(b, ng),
        in_specs=[seq(D_ATTN),
                  pl.BlockSpec((1, pages_per_step, nq, page), lambda i, g, pt: (i, g, 0, 0)),
                  pl.BlockSpec((1, 1, nq, page), lambda i, g, pt: (i, n_pages, 0, 0)),
                  seq(D_ATTN), seq(D_ATTN)] + [page_spec(pp) for pp in range(pages_per_step)] * 2,
        out_specs=seq(D_ATTN),
        scratch_shapes=[pltpu.VMEM((N_HEADS * nq, 1), F32), pltpu.VMEM((N_HEADS * nq, 1), F32),
                        pltpu.VMEM((N_HEADS * nq, HEAD_DIM), F32)],
    )
    return pl.pallas_call(
        functools.partial(_sample_attn_kernel, pages_per_step=pages_per_step),
        grid_spec=grid_spec,
        out_shape=jax.ShapeDtypeStruct((b, nq, D_ATTN), F32),
        compiler_params=_params(),
        name="sample_attention",
    )(page_table.reshape(-1), q, keep, keep, k_new, v_new,
      *([cache_kt] * pages_per_step), *([cache_vt] * pages_per_step))


def _rope_tables(pos):
    half = HEAD_DIM // 2
    freq = ROPE_THETA ** (-jnp.arange(half, dtype=F32) / half)
    ang = pos.astype(F32)[:, None] * freq[None, :]
    cos, sin = jnp.cos(ang), jnp.sin(ang)
    cos = jnp.tile(jnp.concatenate([cos, cos], axis=-1), (1, LANES // HEAD_DIM))
    sin = jnp.tile(jnp.concatenate([-sin, sin], axis=-1), (1, LANES // HEAD_DIM))
    pos1 = jnp.broadcast_to((pos + 1).astype(F32)[:, None], (pos.shape[0], LANES))
    return cos, sin, pos1


def _pack_w_in(w):
    d = w.shape[0]
    o_q, o_k, o_v, o_qi = D_POOL, D_POOL + D_ATTN, D_POOL + 2 * D_ATTN, D_POOL + 3 * D_ATTN
    o_ki = o_qi + IDX_HEADS * IDX_DIM
    o_iw = o_ki + IDX_DIM
    o_g = o_iw + IDX_HEADS
    parts = [w[:, :o_q], w[:, o_q:o_k] * (HEAD_DIM ** -0.5), w[:, o_k:o_qi],
             w[:, o_qi:o_ki], jnp.tile(w[:, o_ki:o_iw], (1, 4)),
             jnp.pad(w[:, o_iw:o_g], ((0, 0), (0, LANES - IDX_HEADS))), w[:, o_g:]]
    packed = jnp.concatenate(parts, axis=1)
    assert packed.shape == (d, C_END)
    return packed.astype(BF16)


def kernel(x_prompt, x_sample, cache_k, cache_v, cache_kidx, state_pool, state_conv, page_table, norm1_g,
           w_in, b_gate, pool_mix_w, pool_scale, w_pool_o, w_attn_o, w_out, norm2_g, w_up, conv_w, conv_b,
           w_down, normf_g):
    depth = w_in.shape[0]
    b, l, d = x_prompt.shape
    db, dl, _ = x_sample.shape
    n_pages = page_table.shape[1]
    page = cache_k.shape[2]
    past = n_pages * page
    c2 = w_up.shape[2]
    assert depth == 1 and d == 1024 and dl % SUBLANES == 0 and l % KEY_CHUNK == 0

    cos_p, sin_p, pos1_p = _rope_tables(jnp.arange(l))
    cos_s, sin_s, pos1_s = _rope_tables(past + jnp.arange(dl))
    topk_p = min(INDEX_TOPK, l // 4)
    topk_s = min(INDEX_TOPK, (past + dl) // 4)
    sample_bt = min(db, 256 // dl)

    cache_kt = jnp.transpose(cache_k, (0, 1, 3, 4, 2))
    cache_vt = jnp.transpose(cache_v, (0, 1, 3, 4, 2))
    cache_kit = jnp.transpose(cache_kidx, (0, 1, 3, 2))

    li = 0
    w_all = _pack_w_in(w_in[li])
    g1 = norm1_g[li][None]
    bg = b_gate[li][None]
    mixw = pool_mix_w[li].astype(BF16)
    pscale = pool_scale[li][None]
    wpo = w_pool_o[li].astype(BF16)
    wao = w_attn_o[li].astype(BF16)
    wout = w_out[li].astype(BF16)
    g2 = norm2_g[li][None]
    wup = w_up[li].astype(BF16)
    wdn = w_down[li].astype(BF16)
    cw = conv_w[li]
    cb = conv_b[li][None]
    gf = normf_g[None]

    (pa, gb, q, qi, misc, ptail, kt, kb, vt, vtb, kit, ki4) = _mixer_in(
        x_prompt, jnp.zeros((b, POOL_PAD, D_POOL), F32), cos_p, sin_p, pos1_p, g1, w_all, bg, mixw, pscale,
        wpo, bt=1, lt=KEY_CHUNK, prompt=True)
    attn = _prompt_attention(qi, misc, q, ki4, kb, vtb, topk=topk_p)
    y_p, ctail = _output_stage(x_prompt, pa, gb, attn, jnp.zeros((b, CONV_PAD, c2), F32), wao, wout, g2,
                               wup, cw, cb, wdn, gf, bt=1, lt=256, ff_chunk=256)
    outs_p = (jnp.transpose(kt.reshape(b, N_HEADS, HEAD_DIM, l), (0, 3, 1, 2))[None],
              jnp.transpose(vt.reshape(b, N_HEADS, HEAD_DIM, l), (0, 3, 1, 2))[None],
              jnp.transpose(kit, (0, 2, 1))[None],
              ptail[:, 1:][None], ctail[:, CONV_PAD - 2:][None])

    pre_pool = jnp.concatenate([jnp.zeros((db, 1, D_POOL), F32), state_pool[li]], axis=1)
    (pa, gb, q, qi, misc, ptail, k, v, ki) = _mixer_in(
        x_sample, pre_pool, cos_s, sin_s, pos1_s, g1, w_all, bg, mixw, pscale, wpo,
        bt=sample_bt, lt=dl, prompt=False)
    keep = _sample_index(page_table, qi, misc, ki, cache_kit, li, topk=topk_s, pages_per_step=32)
    attn = _sample_attention(page_table, q, keep, k, v, cache_kt, cache_vt, li, pages_per_step=16)
    pre_conv = jnp.concatenate([jnp.zeros((db, CONV_PAD - 2, c2), F32), state_conv[li]], axis=1)
    y_s, ctail = _output_stage(x_sample, pa, gb, attn, pre_conv, wao, wout, g2, wup, cw, cb, wdn, gf,
                               bt=sample_bt, lt=dl, ff_chunk=256)
    outs_s = (k.reshape(db, dl, N_HEADS, HEAD_DIM)[None], v.reshape(db, dl, N_HEADS, HEAD_DIM)[None],
              ki[None], ptail[:, 1:][None], ctail[:, CONV_PAD - 2:][None])

    return (y_p, y_s, *outs_p, *outs_s)
```
